```python
import math
import jax, jax.numpy as jnp
from jax import lax
import numpy as np

D_MODEL = 1024
BATCH = 4
SEQ = 4096
DEPTH = 1
DEC_BATCH = 32
DEC_SEQ = 8
PAST_LEN = 8192
PAGE_SIZE = 128

N_HEADS = 8
QK_DIM = 64
V_DIM = 2 * QK_DIM
ATTN_WIDTH = N_HEADS * V_DIM
Q_WIDTH = N_HEADS * 2 * QK_DIM
LRU_WIDTH = D_MODEL
N_LRU_BLOCKS = 8
LRU_BLOCK = LRU_WIDTH // N_LRU_BLOCKS
CONV_WIDTH = 4
LRU_C = 8.0
D_FF = -(-8 * D_MODEL // (3 * 256)) * 256
PLE_DIM = 256
Q_BLOCK = 128
EPS = 1e-6
IN_SIZES = (Q_WIDTH, Q_WIDTH, ATTN_WIDTH, LRU_WIDTH, LRU_WIDTH, D_MODEL, D_MODEL)
IN_COLS = sum(IN_SIZES)

kernel_name = "hybrid_diffattn_rglru_decode_step"


def rmsnorm(x, g):
    xf = x.astype(jnp.float32)
    y = xf * lax.rsqrt(jnp.mean(xf * xf, axis=-1, keepdims=True) + EPS) * g.astype(jnp.float32)
    return y.astype(x.dtype)


def diff_attn(q, k, v, q_pos, k_pos, lam):
    s = jnp.einsum('bqhcd,bkhcd->bchqk', q, k).astype(jnp.float32) * (QK_DIM ** -0.5)
    mask = k_pos[None, :] <= q_pos[:, None]
    s = jnp.where(mask, s, -jnp.inf)
    pr = jax.nn.softmax(s, axis=-1)
    w = pr[:, 0] - lam * pr[:, 1]
    return jnp.einsum('bhqk,bkhd->bqhd', w.astype(v.dtype), v)


def prompt_attend(q, k, v, lam):
    b, t = q.shape[0], q.shape[1]
    nb = t // Q_BLOCK
    qb = q.reshape(b, nb, Q_BLOCK, N_HEADS, 2, QK_DIM).transpose(1, 0, 2, 3, 4, 5)
    k_pos = jnp.arange(t)

    def blk(args):
        i, qi = args
        q_pos = i * Q_BLOCK + jnp.arange(Q_BLOCK)
        return diff_attn(qi, k, v, q_pos, k_pos, lam)

    out = lax.map(blk, (jnp.arange(nb), qb))
    return out.transpose(1, 0, 2, 3, 4).reshape(b, t, N_HEADS, V_DIM)


def make_sample_attend(ck, cv, page_table):
    def attend(q, k, v, lam):
        db, n_pages = page_table.shape
        past = n_pages * PAGE_SIZE
        t = q.shape[1]
        past_k = ck[page_table].reshape(db, past, N_HEADS, 2, QK_DIM)
        past_v = cv[page_table].reshape(db, past, N_HEADS, V_DIM)
        k_all = jnp.concatenate([past_k.astype(k.dtype), k], axis=1)
        v_all = jnp.concatenate([past_v.astype(v.dtype), v], axis=1)
        q_pos = past + jnp.arange(t)
        k_pos = jnp.arange(past + t)
        return diff_attn(q, k_all, v_all, q_pos, k_pos, lam)
    return attend


def causal_conv(x, prev, w, b):
    t = x.shape[1]
    xp = jnp.concatenate([prev.astype(x.dtype), x], axis=1)
    out = b
    for j in range(CONV_WIDTH):
        out = out + xp[:, j:j + t] * w[j]
    return out, xp[:, -(CONV_WIDTH - 1):]


def rglru(x, h0, w_a, b_a, w_x, b_x, lru_lambda):
    b, t, wd = x.shape
    xb = x.reshape(b, t, N_LRU_BLOCKS, LRU_BLOCK)
    r = jax.nn.sigmoid(jnp.einsum('btni,nij->btnj', xb, w_a).astype(jnp.float32) + b_a.astype(jnp.float32)).reshape(b, t, wd)
    ig = jax.nn.sigmoid(jnp.einsum('btni,nij->btnj', xb, w_x).astype(jnp.float32) + b_x.astype(jnp.float32)).reshape(b, t, wd)
    log_a = -LRU_C * r * jax.nn.softplus(-lru_lambda.astype(jnp.float32))
    a = jnp.exp(log_a)
    u = jnp.sqrt(-jnp.expm1(2.0 * log_a)) * (ig * x.astype(jnp.float32))

    def step(h, inp):
        a_t, u_t = inp
        h = a_t * h + u_t
        return h, h

    h_t, hs = lax.scan(step, h0.astype(jnp.float32), (a.transpose(1, 0, 2), u.transpose(1, 0, 2)))
    return hs.transpose(1, 0, 2).astype(x.dtype), h_t


def trunk_layer(h, p_l, attend, conv_prev, h_prev, lw, lam_init):
    b, t, _ = h.shape
    u = rmsnorm(h, lw['g_mix'])
    z = u @ lw['w_in']
    offs = [int(s) for s in np.cumsum(IN_SIZES)[:-1]]
    q, k, v, xr, yr, ga, gr = jnp.split(z, offs, axis=-1)
    q = q.reshape(b, t, N_HEADS, 2, QK_DIM)
    k = k.reshape(b, t, N_HEADS, 2, QK_DIM)
    v = v.reshape(b, t, N_HEADS, V_DIM)
    f32 = jnp.float32
    lam = (jnp.exp(jnp.sum(lw['lq1'].astype(f32) * lw['lk1'].astype(f32)))
           - jnp.exp(jnp.sum(lw['lq2'].astype(f32) * lw['lk2'].astype(f32))) + lam_init)
    o_a = attend(q, k, v, lam)
    o_a = rmsnorm(o_a, lw['g_subln']) * (1.0 - lam_init)
    o_a = o_a.reshape(b, t, ATTN_WIDTH) @ lw['w_attn_br']
    xc, conv_new = causal_conv(xr, conv_prev, lw['w_conv'], lw['b_conv'])
    hs, h_new = rglru(xc, h_prev, lw['w_gate_a'], lw['b_gate_a'], lw['w_gate_x'], lw['b_gate_x'], lw['lru_lambda'])
    o_r = (hs * jax.nn.gelu(yr)) @ lw['w_rec_br']
    m = jax.nn.sigmoid(ga) * o_a + jax.nn.sigmoid(gr) * o_r
    h = h + m @ lw['w_out']
    u2 = rmsnorm(h, lw['g_ffn'])
    h = h + (jax.nn.silu(u2 @ lw['w_ffn_gate']) * (u2 @ lw['w_ffn_up'])) @ lw['w_ffn_down']
    g = jax.nn.sigmoid(rmsnorm(h, lw['g_ple']) @ lw['w_ple_gate'])
    h = h + g * (p_l @ lw['w_ple_proj'])
    return h, k, v, conv_new, h_new.astype(h.dtype)


def setup_inputs(seed: int = 0) -> dict:
    key = jax.random.key(seed)
    ks = iter(jax.random.split(key, 48))
    nrm = lambda shape, s=1.0: s * jax.random.normal(next(ks), shape, jnp.float32)
    n_pages = PAST_LEN // PAGE_SIZE
    n_used = DEC_BATCH * n_pages
    n_phys = (5 * n_used) // 4
    page_table = jax.random.permutation(next(ks), n_phys)[:n_used].reshape(DEC_BATCH, n_pages).astype(jnp.int32)
    gain = lambda shape: 1.0 + nrm(shape, 0.02)
    a0 = jax.random.uniform(next(ks), (DEPTH, LRU_WIDTH), jnp.float32, 0.9, 0.999) ** (1.0 / LRU_C)
    lru_lambda = jnp.log(a0) - jnp.log1p(-a0)
    return {
        'x_prompt': nrm((BATCH, SEQ, D_MODEL)),
        'x_sample': nrm((DEC_BATCH, DEC_SEQ, D_MODEL)),
        'p_prompt': nrm((DEPTH, BATCH, SEQ, PLE_DIM)),
        'p_sample': nrm((DEPTH, DEC_BATCH, DEC_SEQ, PLE_DIM)),
        'cache_k': nrm((DEPTH, n_phys, PAGE_SIZE, N_HEADS, 2, QK_DIM)),
        'cache_v': nrm((DEPTH, n_phys, PAGE_SIZE, N_HEADS, V_DIM)),
        'page_table': page_table,
        'state_conv': nrm((DEPTH, DEC_BATCH, CONV_WIDTH - 1, LRU_WIDTH)),
        'state_h': nrm((DEPTH, DEC_BATCH, LRU_WIDTH), 0.5),
        'g_mix': gain((DEPTH, D_MODEL)),
        'w_in': nrm((DEPTH, D_MODEL, IN_COLS), D_MODEL ** -0.5),
        'lambda_q1': nrm((DEPTH, QK_DIM), 0.1),
        'lambda_k1': nrm((DEPTH, QK_DIM), 0.1),
        'lambda_q2': nrm((DEPTH, QK_DIM), 0.1),
        'lambda_k2': nrm((DEPTH, QK_DIM), 0.1),
        'g_subln': gain((DEPTH, V_DIM)),
        'w_attn_br': nrm((DEPTH, ATTN_WIDTH, D_MODEL), ATTN_WIDTH ** -0.5),
        'w_conv': nrm((DEPTH, CONV_WIDTH, LRU_WIDTH), CONV_WIDTH ** -0.5),
        'b_conv': nrm((DEPTH, LRU_WIDTH), 0.01),
        'w_gate_a': nrm((DEPTH, N_LRU_BLOCKS, LRU_BLOCK, LRU_BLOCK), LRU_BLOCK ** -0.5),
        'b_gate_a': nrm((DEPTH, N_LRU_BLOCKS, LRU_BLOCK), 0.1),
        'w_gate_x': nrm((DEPTH, N_LRU_BLOCKS, LRU_BLOCK, LRU_BLOCK), LRU_BLOCK ** -0.5),
        'b_gate_x': nrm((DEPTH, N_LRU_BLOCKS, LRU_BLOCK), 0.1),
        'lru_lambda': lru_lambda,
        'w_rec_br': nrm((DEPTH, LRU_WIDTH, D_MODEL), LRU_WIDTH ** -0.5),
        'w_out': nrm((DEPTH, D_MODEL, D_MODEL), D_MODEL ** -0.5),
        'g_ffn': gain((DEPTH, D_MODEL)),
        'w_ffn_gate': nrm((DEPTH, D_MODEL, D_FF), D_MODEL ** -0.5),
        'w_ffn_up': nrm((DEPTH, D_MODEL, D_FF), D_MODEL ** -0.5),
        'w_ffn_down': nrm((DEPTH, D_FF, D_MODEL), D_FF ** -0.5),
        'g_ple': gain((DEPTH, D_MODEL)),
        'w_ple_gate': nrm((DEPTH, D_MODEL, D_MODEL), D_MODEL ** -0.5),
        'w_ple_proj': nrm((DEPTH, PLE_DIM, D_MODEL), PLE_DIM ** -0.5),
        'g_final': gain((D_MODEL,)),
    }


def reference(x_prompt, x_sample, p_prompt, p_sample, cache_k, cache_v, page_table, state_conv, state_h,
              g_mix, w_in, lambda_q1, lambda_k1, lambda_q2, lambda_k2, g_subln, w_attn_br,
              w_conv, b_conv, w_gate_a, b_gate_a, w_gate_x, b_gate_x, lru_lambda, w_rec_br,
              w_out, g_ffn, w_ffn_gate, w_ffn_up, w_ffn_down, g_ple, w_ple_gate, w_ple_proj, g_final):
    hp, hs = x_prompt, x_sample
    kp_l, vp_l, cp_l, hp_l, ks_l, vs_l, cs_l, hs_l = [], [], [], [], [], [], [], []
    bp = x_prompt.shape[0]
    for l in range(DEPTH):
        lw = dict(g_mix=g_mix[l], w_in=w_in[l], lq1=lambda_q1[l], lk1=lambda_k1[l], lq2=lambda_q2[l],
                  lk2=lambda_k2[l], g_subln=g_subln[l], w_attn_br=w_attn_br[l], w_conv=w_conv[l],
                  b_conv=b_conv[l], w_gate_a=w_gate_a[l], b_gate_a=b_gate_a[l], w_gate_x=w_gate_x[l],
                  b_gate_x=b_gate_x[l], lru_lambda=lru_lambda[l], w_rec_br=w_rec_br[l], w_out=w_out[l],
                  g_ffn=g_ffn[l], w_ffn_gate=w_ffn_gate[l], w_ffn_up=w_ffn_up[l], w_ffn_down=w_ffn_down[l],
                  g_ple=g_ple[l], w_ple_gate=w_ple_gate[l], w_ple_proj=w_ple_proj[l])
        lam_init = 0.8 - 0.6 * math.exp(-0.3 * l)
        conv0 = jnp.zeros((bp, CONV_WIDTH - 1, LRU_WIDTH), x_prompt.dtype)
        h0 = jnp.zeros((bp, LRU_WIDTH), jnp.float32)
        hp, kp, vp, cp, hpn = trunk_layer(hp, p_prompt[l], prompt_attend, conv0, h0, lw, lam_init)
        attend_s = make_sample_attend(cache_k[l], cache_v[l], page_table)
        hs, ksn, vsn, csn, hsn = trunk_layer(hs, p_sample[l], attend_s, state_conv[l], state_h[l], lw, lam_init)
        kp_l.append(kp); vp_l.append(vp); cp_l.append(cp); hp_l.append(hpn)
        ks_l.append(ksn); vs_l.append(vsn); cs_l.append(csn); hs_l.append(hsn)
    y_prompt = rmsnorm(hp, g_final)
    y_sample = rmsnorm(hs, g_final)
    return (y_prompt, y_sample,
            jnp.stack(kp_l), jnp.stack(vp_l), jnp.stack(cp_l), jnp.stack(hp_l),
            jnp.stack(ks_l), jnp.stack(vs_l), jnp.stack(cs_l), jnp.stack(hs_l))
```

```python
import functools
import math

import jax
import jax.numpy as jnp
from jax import lax
from jax.experimental import pallas as pl
from jax.experimental.pallas import tpu as pltpu

F32 = jnp.float32
BF16 = jnp.bfloat16

EPS = 1e-6
LRU_C = 8.0
CONV_WIDTH = 4
N_LRU_BLOCKS = 8
SUBLANES = 8
LANES = 128
CONV_PAD = 8
VMEM_LIMIT = 56 * 1024 * 1024


def _rms(x, g):
    ms = jnp.mean(x * x, axis=-1, keepdims=True)
    return x * lax.rsqrt(ms + EPS) * g


def _softplus(x):
    return jnp.maximum(x, 0.0) + jnp.log1p(jnp.exp(-jnp.abs(x)))


def _resident(shape):
    nd = len(shape)
    return pl.BlockSpec(shape, lambda *_: (0,) * nd, pipeline_mode=pl.Buffered(1))


def _mix_in_kernel(x_ref, cprev_ref, hprev_ref, gmix_ref, win_ref, wconv_ref, bconv_ref,
                   wgate_ref, bgate_ref, lam_ref,
                   q_ref, kf_ref, vf_ref, kb_ref, vb_ref, rin_ref, sga_ref, sgr_ref,
                   cnew_ref, hnew_ref,
                   xbuf, abuf, ubuf, hstate, *, nb, tt, width, qk_scale):
    t = pl.program_id(1)
    tm = nb * tt
    lru_block = width // N_LRU_BLOCKS

    @pl.when(t == 0)
    def _():
        xbuf[:, CONV_PAD - (CONV_WIDTH - 1):CONV_PAD, :] = cprev_ref[...]
        hstate[...] = hprev_ref[...]

    u = _rms(x_ref[...], gmix_ref[...]).astype(BF16)

    def proj(i):
        return jnp.dot(u, win_ref[:, i * width:(i + 1) * width], preferred_element_type=F32)

    q_ref[...] = (proj(0) * qk_scale).astype(BF16)
    k = proj(1)
    kf_ref[...] = k
    kb_ref[...] = k.astype(BF16)
    v = proj(2)
    vf_ref[...] = v
    vb_ref[...] = v.astype(BF16)

    xbuf[:, CONV_PAD:CONV_PAD + tt, :] = proj(3).reshape(nb, tt, width)
    xc = bconv_ref[...].reshape(1, 1, width)
    for j in range(CONV_WIDTH):
        r0 = CONV_PAD - (CONV_WIDTH - 1) + j
        xc = xc + xbuf[:, r0:r0 + tt, :] * wconv_ref[j:j + 1, :].reshape(1, 1, width)
    tail = xbuf[:, CONV_PAD + tt - (CONV_WIDTH - 1):CONV_PAD + tt, :]
    cnew_ref[...] = tail
    xbuf[:, CONV_PAD - (CONV_WIDTH - 1):CONV_PAD, :] = tail

    xc = xc.reshape(tm, width)
    xcb = xc.astype(BF16)
    sp = _softplus(-lam_ref[...])
    for blk in range(N_LRU_BLOCKS):
        sl = slice(blk * lru_block, (blk + 1) * lru_block)
        g = jnp.dot(xcb[:, sl], wgate_ref[blk], preferred_element_type=F32) + bgate_ref[blk]
        r = jax.nn.sigmoid(g[:, :lru_block])
        ig = jax.nn.sigmoid(g[:, lru_block:])
        a = jnp.exp(-LRU_C * r * sp[:, sl])
        abuf[:, sl] = a
        ubuf[:, sl] = jnp.sqrt(1.0 - a * a) * (ig * xc[:, sl])

    n_groups = tt // SUBLANES
    row = lax.broadcasted_iota(jnp.int32, (SUBLANES, width), 0)

    def batch_body(b, carry):
        def group_body(g, h):
            r0 = pl.multiple_of((b * n_groups + g) * SUBLANES, SUBLANES)
            a = abuf[pl.ds(r0, SUBLANES), :]
            uu = ubuf[pl.ds(r0, SUBLANES), :]
            d = 1
            while d < SUBLANES:
                a_sh = jnp.where(row >= d, pltpu.roll(a, d, 0), 1.0)
                u_sh = jnp.where(row >= d, pltpu.roll(uu, d, 0), 0.0)
                uu = a * u_sh + uu
                a = a * a_sh
                d *= 2
            hb = uu + a * h
            ubuf[pl.ds(r0, SUBLANES), :] = hb
            return hb[SUBLANES - 1:SUBLANES, :]

        h = lax.fori_loop(0, n_groups, group_body, hstate[b], unroll=min(n_groups, 4))
        hstate[b] = h
        return carry

    lax.fori_loop(0, nb, batch_body, 0)
    hnew_ref[...] = hstate[...]

    rin_ref[...] = (ubuf[...] * jax.nn.gelu(proj(4))).astype(BF16)
    sga_ref[...] = jax.nn.sigmoid(proj(5))
    sgr_ref[...] = jax.nn.sigmoid(proj(6))


def _mix_in(x2d, conv_prev, h_prev, lw, *, batch, seq, nb, tt, qk_scale):
    n_tok, d_model = x2d.shape
    width = lw['w_conv'].shape[-1]
    n_t = seq // tt
    tm = nb * tt
    grid = (batch // nb, n_t)
    tok = lambda dt: jax.ShapeDtypeStruct((n_tok, width), dt)
    tok_spec = pl.BlockSpec((tm, width), lambda b, t: (b * n_t + t, 0))
    state_c = pl.BlockSpec((nb, CONV_WIDTH - 1, width), lambda b, t: (b, 0, 0))
    state_h = pl.BlockSpec((nb, 1, width), lambda b, t: (b, 0, 0))
    kern = functools.partial(_mix_in_kernel, nb=nb, tt=tt, width=width, qk_scale=qk_scale)
    return pl.pallas_call(
        kern,
        grid=grid,
        in_specs=[
            pl.BlockSpec((tm, d_model), lambda b, t: (b * n_t + t, 0)),
            state_c, state_h,
            _resident((1, d_model)),
            _resident(lw['w_in'].shape),
            _resident((CONV_WIDTH, width)),
            _resident((1, width)),
            _resident(lw['w_gate'].shape),
            _resident(lw['b_gate'].shape),
            _resident((1, width)),
        ],
        out_specs=[tok_spec] * 8 + [state_c, state_h],
        out_shape=[tok(BF16), tok(F32), tok(F32), tok(BF16), tok(BF16), tok(BF16), tok(F32), tok(F32),
                   jax.ShapeDtypeStruct((batch, CONV_WIDTH - 1, width), F32),
                   jax.ShapeDtypeStruct((batch, 1, width), F32)],
        scratch_shapes=[
            pltpu.VMEM((nb, CONV_PAD + tt, width), F32),
            pltpu.VMEM((tm, width), F32),
            pltpu.VMEM((tm, width), F32),
            pltpu.VMEM((nb, 1, width), F32),
        ],
        compiler_params=pltpu.CompilerParams(
            dimension_semantics=("arbitrary", "arbitrary"), vmem_limit_bytes=VMEM_LIMIT),
        name="mix_in",
    )(x2d, conv_prev, h_prev, lw['g_mix'], lw['w_in'], lw['w_conv'], lw['b_conv'],
      lw['w_gate'], lw['b_gate'], lw['lru_lambda'])


def _diff_lambda(lq1_ref, lk1_ref, lq2_ref, lk2_ref, lam_init):
    s1 = jnp.sum(lq1_ref[...] * lk1_ref[...], axis=-1, keepdims=True)
    s2 = jnp.sum(lq2_ref[...] * lk2_ref[...], axis=-1, keepdims=True)
    return jnp.exp(s1) - jnp.exp(s2) + lam_init


def _head_out(o1, o2, lam, gsub, lam_init):
    od = o1 - lam * o2
    return _rms(od, gsub) * (1.0 - lam_init)


def _prompt_attn_kernel(q_ref, k_ref, v_ref, lq1_ref, lk1_ref, lq2_ref, lk2_ref, gsub_ref, o_ref,
                        *, tq, qk_dim, lam_init):
    i = pl.program_id(2)
    v_dim = v_ref.shape[-1]
    rows = 2 * tq
    q = q_ref[...]
    lane = lax.broadcasted_iota(jnp.int32, q.shape, 1)
    zero = jnp.zeros_like(q)
    qbd = jnp.concatenate([jnp.where(lane < qk_dim, q, zero), jnp.where(lane >= qk_dim, q, zero)], axis=0)

    def step(j, carry, masked):
        m, l, acc = carry
        k0 = pl.multiple_of(j * tq, tq)
        kb = k_ref[pl.ds(k0, tq), :]
        vb = v_ref[pl.ds(k0, tq), :]
        s = lax.dot_general(qbd, kb, (((1,), (1,)), ((), ())), preferred_element_type=F32)
        if masked:
            r = lax.broadcasted_iota(jnp.int32, (rows, tq), 0)
            c = lax.broadcasted_iota(jnp.int32, (rows, tq), 1)
            qpos = jnp.where(r >= tq, r - tq, r)
            s = jnp.where(c <= qpos, s, -jnp.inf)
        m_new = jnp.maximum(m, jnp.max(s, axis=-1, keepdims=True))
        alpha = jnp.exp(m - m_new)
        p = jnp.exp(s - m_new)
        l = alpha * l + jnp.sum(p, axis=-1, keepdims=True)
        acc = alpha * acc + jnp.dot(p.astype(BF16), vb, preferred_element_type=F32)
        return m_new, l, acc

    init = (jnp.full((rows, 1), -jnp.inf, F32), jnp.zeros((rows, 1), F32), jnp.zeros((rows, v_dim), F32))
    carry = lax.fori_loop(0, i, lambda j, c: step(j, c, False), init)
    _, l, acc = step(i, carry, True)
    o = acc / l
    lam = _diff_lambda(lq1_ref, lk1_ref, lq2_ref, lk2_ref, lam_init)
    o_ref[...] = _head_out(o[:tq], o[tq:], lam, gsub_ref[...], lam_init).astype(o_ref.dtype)


def _prompt_attention(q, k, v, lw, *, n_heads, qk_dim, lam_init, tq):
    batch, seq, _ = q.shape
    v_dim = v.shape[-1] // n_heads
    assert 2 * qk_dim == LANES and v_dim == LANES
    small = lambda n: _resident((1, n))
    kern = functools.partial(_prompt_attn_kernel, tq=tq, qk_dim=qk_dim, lam_init=lam_init)
    return pl.pallas_call(
        kern,
        grid=(batch, n_heads, seq // tq),
        in_specs=[
            pl.BlockSpec((None, tq, 2 * qk_dim), lambda b, h, i: (b, i, h)),
            pl.BlockSpec((None, seq, 2 * qk_dim), lambda b, h, i: (b, 0, h)),
            pl.BlockSpec((None, seq, v_dim), lambda b, h, i: (b, 0, h)),
            small(qk_dim), small(qk_dim), small(qk_dim), small(qk_dim), small(v_dim),
        ],
        out_specs=pl.BlockSpec((None, tq, v_dim), lambda b, h, i: (b, i, h)),
        out_shape=jax.ShapeDtypeStruct((batch, seq, n_heads * v_dim), BF16),
        compiler_params=pltpu.CompilerParams(
            dimension_semantics=("arbitrary", "arbitrary", "arbitrary"), vmem_limit_bytes=VMEM_LIMIT),
        name="prompt_attn",
    )(q, k, v, lw['lq1'], lw['lk1'], lw['lq2'], lw['lk2'], lw['g_subln'])


def _sample_attn_kernel(pt_ref, q_ref, kn_ref, vn_ref, *rest, n_pg, n_heads, qk_dim, page, lam_init):
    k_refs = rest[:n_pg]
    v_refs = rest[n_pg:2 * n_pg]
    lq1_ref, lk1_ref, lq2_ref, lk2_ref, gsub_ref, o_ref, qbd_ref, kbuf, vbuf, m_ref, l_ref, acc_ref = rest[2 * n_pg:]
    s_idx = pl.program_id(1)
    tq, feat = q_ref.shape
    n_maps = 2 * n_heads
    rows = n_maps * tq
    v_dim = feat // n_heads

    @pl.when(s_idx == 0)
    def _():
        qt = jnp.concatenate([q_ref[...]] * n_maps, axis=0)
        r = lax.broadcasted_iota(jnp.int32, (rows, feat), 0)
        c = lax.broadcasted_iota(jnp.int32, (rows, feat), 1)
        qbd = jnp.where(r // tq == c // qk_dim, qt, 0.0).astype(BF16)
        qbd_ref[...] = qbd
        pad = jnp.zeros((page - tq, feat), F32)
        kn = jnp.concatenate([kn_ref[...], pad], axis=0).astype(BF16)
        vn = jnp.concatenate([vn_ref[...], pad], axis=0).astype(BF16)
        s = lax.dot_general(qbd, kn, (((1,), (1,)), ((), ())), preferred_element_type=F32)
        rr = lax.broadcasted_iota(jnp.int32, (rows, page), 0)
        cc = lax.broadcasted_iota(jnp.int32, (rows, page), 1)
        s = jnp.where(cc <= rr % tq, s, -jnp.inf)
        m = jnp.max(s, axis=-1, keepdims=True)
        p = jnp.exp(s - m)
        m_ref[...] = m
        l_ref[...] = jnp.sum(p, axis=-1, keepdims=True)
        acc_ref[...] = jnp.dot(p.astype(BF16), vn, preferred_element_type=F32)

    for g in range(n_pg):
        kbuf[:, g * page:(g + 1) * page] = k_refs[g][...].astype(BF16)
        for h in range(n_heads):
            vh = v_refs[g][pl.ds(h, page, stride=n_heads), :]
            vbuf[g * page:(g + 1) * page, h * v_dim:(h + 1) * v_dim] = vh.astype(BF16)

    s = jnp.dot(qbd_ref[...], kbuf[...], preferred_element_type=F32)
    m = m_ref[...]
    m_new = jnp.maximum(m, jnp.max(s, axis=-1, keepdims=True))
    alpha = jnp.exp(m - m_new)
    p = jnp.exp(s - m_new)
    l_ref[...] = alpha * l_ref[...] + jnp.sum(p, axis=-1, keepdims=True)
    acc_ref[...] = alpha * acc_ref[...] + jnp.dot(p.astype(BF16), vbuf[...], preferred_element_type=F32)
    m_ref[...] = m_new

    @pl.when(s_idx == pl.num_programs(1) - 1)
    def _():
        lam = _diff_lambda(lq1_ref, lk1_ref, lq2_ref, lk2_ref, lam_init)
        o = acc_ref[...] / l_ref[...]
        for h in range(n_heads):
            cols = slice(h * v_dim, (h + 1) * v_dim)
            o1 = o[(2 * h) * tq:(2 * h + 1) * tq, cols]
            o2 = o[(2 * h + 1) * tq:(2 * h + 2) * tq, cols]
            o_ref[:, cols] = _head_out(o1, o2, lam, gsub_ref[...], lam_init)


def _sample_attention(q, k_new, v_new, kt_pages, v_pages, page_table, lw, *, n_heads, qk_dim, lam_init, n_pg):
    dec_batch, tq, feat = q.shape
    page = kt_pages.shape[-1]
    n_pages = page_table.shape[1]
    v_dim = feat // n_heads
    rows = 2 * n_heads * tq
    assert n_pages % n_pg == 0 and rows % SUBLANES == 0 and v_dim == LANES and page == LANES
    small = lambda n: pl.BlockSpec((1, n), lambda b, s, pt: (0, 0))
    new_spec = pl.BlockSpec((None, tq, feat), lambda b, s, pt: (b, 0, 0))

    def page_spec(g, shape):
        return pl.BlockSpec((None,) + shape, lambda b, s, pt: (pt[b, s * n_pg + g], 0, 0))

    kern = functools.partial(_sample_attn_kernel, n_pg=n_pg, n_heads=n_heads, qk_dim=qk_dim, page=page,
                             lam_init=lam_init)
    grid_spec = pltpu.PrefetchScalarGridSpec(
        num_scalar_prefetch=1,
        grid=(dec_batch, n_pages // n_pg),
        in_specs=[new_spec, new_spec, new_spec]
        + [page_spec(g, (feat, page)) for g in range(n_pg)]
        + [page_spec(g, (page * n_heads, v_dim)) for g in range(n_pg)]
        + [small(qk_dim)] * 4 + [small(v_dim)],
        out_specs=new_spec,
        scratch_shapes=[
            pltpu.VMEM((rows, feat), BF16),
            pltpu.VMEM((feat, n_pg * page), BF16),
            pltpu.VMEM((n_pg * page, feat), BF16),
            pltpu.VMEM((rows, 1), F32),
            pltpu.VMEM((rows, 1), F32),
            pltpu.VMEM((rows, feat), F32),
        ],
    )
    return pl.pallas_call(
        kern,
        grid_spec=grid_spec,
        out_shape=jax.ShapeDtypeStruct((dec_batch, tq, feat), F32),
        compiler_params=pltpu.CompilerParams(
            dimension_semantics=("arbitrary", "arbitrary"), vmem_limit_bytes=VMEM_LIMIT),
        name="sample_attn",
    )(page_table, q, k_new, v_new, *([kt_pages] * n_pg), *([v_pages] * n_pg),
      lw['lq1'], lw['lk1'], lw['lq2'], lw['lk2'], lw['g_subln'])


def _mix_out_kernel(x_ref, oa_ref, rin_ref, sga_ref, sgr_ref, p_ref,
                    wattn_ref, wrec_ref, wout_ref, gffn_ref, wg_ref, wu_ref, wd_ref,
                    gple_ref, wpg_ref, wpp_ref, gfin_ref, y_ref, *, ff_chunk, final_norm):
    dot = functools.partial(jnp.dot, preferred_element_type=F32)
    o_a = dot(oa_ref[...].astype(BF16), wattn_ref[...])
    o_r = dot(rin_ref[...], wrec_ref[...])
    m = sga_ref[...] * o_a + sgr_ref[...] * o_r
    h = x_ref[...] + dot(m.astype(BF16), wout_ref[...])

    u2 = _rms(h, gffn_ref[...]).astype(BF16)
    d_ff = wg_ref.shape[1]
    ffn = jnp.zeros_like(h)
    for c0 in range(0, d_ff, ff_chunk):
        c1 = min(c0 + ff_chunk, d_ff)
        hid = jax.nn.silu(dot(u2, wg_ref[:, c0:c1])) * dot(u2, wu_ref[:, c0:c1])
        ffn = ffn + dot(hid.astype(BF16), wd_ref[c0:c1, :])
    h = h + ffn

    g = jax.nn.sigmoid(dot(_rms(h, gple_ref[...]).astype(BF16), wpg_ref[...]))
    h = h + g * dot(p_ref[...].astype(BF16), wpp_ref[...])
    y_ref[...] = _rms(h, gfin_ref[...]) if final_norm else h


def _mix_out(x2d, oa, rin, sga, sgr, p2d, lw, g_final, *, tm, final_norm):
    n_tok, d_model = x2d.shape
    tok_spec = lambda w: pl.BlockSpec((tm, w), lambda i: (i, 0))
    weights = [lw['w_attn_br'], lw['w_rec_br'], lw['w_out'], lw['g_ffn'], lw['w_ffn_gate'], lw['w_ffn_up'],
               lw['w_ffn_down'], lw['g_ple'], lw['w_ple_gate'], lw['w_ple_proj'], g_final]
    kern = functools.partial(_mix_out_kernel, ff_chunk=1024, final_norm=final_norm)
    return pl.pallas_call(
        kern,
        grid=(n_tok // tm,),
        in_specs=[tok_spec(d_model), tok_spec(oa.shape[1]), tok_spec(rin.shape[1]), tok_spec(d_model),
                  tok_spec(d_model), tok_spec(p2d.shape[1])] + [_resident(w.shape) for w in weights],
        out_specs=tok_spec(d_model),
        out_shape=jax.ShapeDtypeStruct((n_tok, d_model), F32),
        compiler_params=pltpu.CompilerParams(
            dimension_semantics=("arbitrary",), vmem_limit_bytes=VMEM_LIMIT),
        name="mix_out",
    )(x2d, oa, rin, sga, sgr, p2d, *weights)


def kernel(x_prompt, x_sample, p_prompt, p_sample, cache_k, cache_v, page_table, state_conv, state_h, g_mix, w_in, lambda_q1, lambda_k1, lambda_q2, lambda_k2, g_subln, w_attn_br, w_conv, b_conv, w_gate_a, b_gate_a, w_gate_x, b_gate_x, lru_lambda, w_rec_br, w_out, g_ffn, w_ffn_gate, w_ffn_up, w_ffn_down, g_ple, w_ple_gate, w_ple_proj, g_final):
    depth = w_in.shape[0]
    batch, seq, d_model = x_prompt.shape
    dec_batch, dec_seq, _ = x_sample.shape
    _, n_phys, page, n_heads, _, qk_dim = cache_k.shape
    v_dim = cache_v.shape[-1]
    width = w_conv.shape[-1]
    feat = n_heads * v_dim
    qk_scale = qk_dim ** -0.5

    hp = x_prompt.reshape(batch * seq, d_model)
    hs = x_sample.reshape(dec_batch * dec_seq, d_model)
    outs = [[] for _ in range(8)]
    for l in range(depth):
        lam_init = 0.8 - 0.6 * math.exp(-0.3 * l)
        last = l == depth - 1
        lw = dict(
            g_mix=g_mix[l][None], w_in=w_in[l].astype(BF16),
            lq1=lambda_q1[l][None], lk1=lambda_k1[l][None], lq2=lambda_q2[l][None], lk2=lambda_k2[l][None],
            g_subln=g_subln[l][None], w_attn_br=w_attn_br[l].astype(BF16),
            w_conv=w_conv[l], b_conv=b_conv[l][None],
            w_gate=jnp.concatenate([w_gate_a[l], w_gate_x[l]], axis=-1).astype(BF16),
            b_gate=jnp.concatenate([b_gate_a[l], b_gate_x[l]], axis=-1)[:, None, :],
            lru_lambda=lru_lambda[l][None], w_rec_br=w_rec_br[l].astype(BF16), w_out=w_out[l].astype(BF16),
            g_ffn=g_ffn[l][None], w_ffn_gate=w_ffn_gate[l].astype(BF16), w_ffn_up=w_ffn_up[l].astype(BF16),
            w_ffn_down=w_ffn_down[l].astype(BF16), g_ple=g_ple[l][None],
            w_ple_gate=w_ple_gate[l].astype(BF16), w_ple_proj=w_ple_proj[l].astype(BF16))
        gfin = g_final[None]

        conv0 = jnp.zeros((batch, CONV_WIDTH - 1, width), F32)
        h0 = jnp.zeros((batch, 1, width), F32)
        q, kf, vf, kb, vb, rin, sga, sgr, cnew, hnew = _mix_in(
            hp, conv0, h0, lw, batch=batch, seq=seq, nb=1, tt=256, qk_scale=qk_scale)
        oa = _prompt_attention(q.reshape(batch, seq, feat), kb.reshape(batch, seq, feat),
                               vb.reshape(batch, seq, feat), lw,
                               n_heads=n_heads, qk_dim=qk_dim, lam_init=lam_init, tq=256)
        hp = _mix_out(hp, oa.reshape(batch * seq, feat), rin, sga, sgr,
                      p_prompt[l].reshape(batch * seq, -1), lw, gfin, tm=256, final_norm=last)
        outs[0].append(kf.reshape(batch, seq, n_heads, 2, qk_dim))
        outs[1].append(vf.reshape(batch, seq, n_heads, v_dim))
        outs[2].append(cnew)
        outs[3].append(hnew.reshape(batch, width))

        q, kf, vf, kb, vb, rin, sga, sgr, cnew, hnew = _mix_in(
            hs, state_conv[l], state_h[l][:, None, :], lw,
            batch=dec_batch, seq=dec_seq, nb=dec_batch, tt=dec_seq, qk_scale=qk_scale)
        kt_pages = jnp.transpose(cache_k[l], (0, 2, 3, 4, 1)).reshape(n_phys, feat, page)
        v_pages = cache_v[l].reshape(n_phys, page * n_heads, v_dim)
        oa = _sample_attention(q.astype(F32).reshape(dec_batch, dec_seq, feat),
                               kf.reshape(dec_batch, dec_seq, feat), vf.reshape(dec_batch, dec_seq, feat),
                               kt_pages, v_pages, page_table, lw,
                               n_heads=n_heads, qk_dim=qk_dim, lam_init=lam_init, n_pg=8)
        hs = _mix_out(hs, oa.reshape(dec_batch * dec_seq, feat), rin, sga, sgr,
                      p_sample[l].reshape(dec_batch * dec_seq, -1), lw, gfin,
                      tm=dec_batch * dec_seq, final_norm=last)
        outs[4].append(kf.reshape(dec_batch, dec_seq, n_heads, 2, qk_dim))
        outs[5].append(vf.reshape(dec_batch, dec_seq, n_heads, v_dim))
        outs[6].append(cnew)
        outs[7].append(hnew.reshape(dec_batch, width))

    y_prompt = hp.reshape(batch, seq, d_model)
    y_sample = hs.reshape(dec_batch, dec_seq, d_model)
    return (y_prompt, y_sample) + tuple(jnp.stack(o) for o in outs)
```

```python
import functools
import math

import jax
import jax.numpy as jnp
from jax import lax
from jax.experimental import pallas as pl
from jax.experimental.pallas import tpu as pltpu

F32 = jnp.float32
BF16 = jnp.bfloat16

EPS = 1e-6
LRU_C = 8.0
CONV_WIDTH = 4
N_LRU_BLOCKS = 8
SUBLANES = 8
LANES = 128
CONV_PAD = 8
VMEM_LIMIT = 56 * 1024 * 1024


def _rms(x, g):
    ms = jnp.mean(x * x, axis=-1, keepdims=True)
    return x * lax.rsqrt(ms + EPS) * g


def _softplus(x):
    return jnp.maximum(x, 0.0) + jnp.log1p(jnp.exp(-jnp.abs(x)))


def _resident(shape):
    nd = len(shape)
    return pl.BlockSpec(shape, lambda *_: (0,) * nd, pipeline_mode=pl.Buffered(1))


def _mix_in_kernel(x_ref, cprev_ref, hprev_ref, gmix_ref, win_ref, wkt_ref, wconv_ref, bconv_ref,
                   wgate_ref, bgate_ref, lam_ref,
                   q_ref, kf_ref, vf_ref, kb_ref, vb_ref, rin_ref, sga_ref, sgr_ref,
                   cnew_ref, hnew_ref,
                   xbuf, abuf, ubuf, hstate, *, nb, tt, width, qk_scale, k_transposed):
    t = pl.program_id(1)
    tm = nb * tt
    lru_block = width // N_LRU_BLOCKS

    @pl.when(t == 0)
    def _():
        xbuf[:, CONV_PAD - (CONV_WIDTH - 1):CONV_PAD, :] = cprev_ref[...]
        hstate[...] = hprev_ref[...]

    u = _rms(x_ref[...], gmix_ref[...]).astype(BF16)

    def proj(i):
        return jnp.dot(u, win_ref[:, i * width:(i + 1) * width], preferred_element_type=F32)

    q_ref[...] = (proj(0) * qk_scale).astype(BF16)
    if k_transposed:
        k = lax.dot_general(wkt_ref[...], u, (((1,), (1,)), ((), ())), preferred_element_type=F32)
    else:
        k = proj(1)
    kf_ref[...] = k
    kb_ref[...] = k.astype(BF16)
    v = proj(2)
    vf_ref[...] = v
    vb_ref[...] = v.astype(BF16)

    xbuf[:, CONV_PAD:CONV_PAD + tt, :] = proj(3).reshape(nb, tt, width)
    xc = bconv_ref[...].reshape(1, 1, width)
    for j in range(CONV_WIDTH):
        r0 = CONV_PAD - (CONV_WIDTH - 1) + j
        xc = xc + xbuf[:, r0:r0 + tt, :] * wconv_ref[j:j + 1, :].reshape(1, 1, width)
    tail = xbuf[:, CONV_PAD + tt - (CONV_WIDTH - 1):CONV_PAD + tt, :]
    cnew_ref[...] = tail
    xbuf[:, CONV_PAD - (CONV_WIDTH - 1):CONV_PAD, :] = tail

    xc = xc.reshape(tm, width)
    xcb = xc.astype(BF16)
    sp = _softplus(-lam_ref[...])
    for blk in range(N_LRU_BLOCKS):
        sl = slice(blk * lru_block, (blk + 1) * lru_block)
        g = jnp.dot(xcb[:, sl], wgate_ref[blk], preferred_element_type=F32) + bgate_ref[blk]
        r = jax.nn.sigmoid(g[:, :lru_block])
        ig = jax.nn.sigmoid(g[:, lru_block:])
        a = jnp.exp(-LRU_C * r * sp[:, sl])
        abuf[:, sl] = a
        ubuf[:, sl] = jnp.sqrt(1.0 - a * a) * (ig * xc[:, sl])

    n_groups = tt // SUBLANES
    row = lax.broadcasted_iota(jnp.int32, (SUBLANES, width), 0)

    for b in range(nb):
        h = hstate[b]
        for g in range(n_groups):
            r0 = (b * n_groups + g) * SUBLANES
            a = abuf[r0:r0 + SUBLANES, :]
            uu = ubuf[r0:r0 + SUBLANES, :]
            d = 1
            while d < SUBLANES:
                a_sh = jnp.where(row >= d, pltpu.roll(a, d, 0), 1.0)
                u_sh = jnp.where(row >= d, pltpu.roll(uu, d, 0), 0.0)
                uu = a * u_sh + uu
                a = a * a_sh
                d *= 2
            hb = uu + a * h
            ubuf[r0:r0 + SUBLANES, :] = hb
            h = hb[SUBLANES - 1:SUBLANES, :]
        hstate[b] = h
    hnew_ref[...] = hstate[...]

    rin_ref[...] = (ubuf[...] * jax.nn.gelu(proj(4))).astype(BF16)
    sga_ref[...] = jax.nn.sigmoid(proj(5))
    sgr_ref[...] = jax.nn.sigmoid(proj(6))


def _mix_in(x2d, conv_prev, h_prev, lw, *, batch, seq, nb, tt, qk_scale, k_transposed):
    n_tok, d_model = x2d.shape
    width = lw['w_conv'].shape[-1]
    n_t = seq // tt
    tm = nb * tt
    grid = (batch // nb, n_t)
    tok = lambda dt: jax.ShapeDtypeStruct((n_tok, width), dt)
    tok_spec = pl.BlockSpec((tm, width), lambda b, t: (b * n_t + t, 0))
    if k_transposed:
        assert nb == 1
        k_shape = lambda dt: jax.ShapeDtypeStruct((batch, width, seq), dt)
        k_spec = pl.BlockSpec((None, width, tt), lambda b, t: (b, 0, t))
    else:
        k_shape, k_spec = tok, tok_spec
    state_c = pl.BlockSpec((nb, CONV_WIDTH - 1, width), lambda b, t: (b, 0, 0))
    state_h = pl.BlockSpec((nb, 1, width), lambda b, t: (b, 0, 0))
    kern = functools.partial(_mix_in_kernel, nb=nb, tt=tt, width=width, qk_scale=qk_scale,
                             k_transposed=k_transposed)
    return pl.pallas_call(
        kern,
        grid=grid,
        in_specs=[
            pl.BlockSpec((tm, d_model), lambda b, t: (b * n_t + t, 0)),
            state_c, state_h,
            _resident((1, d_model)),
            _resident(lw['w_in'].shape),
            _resident(lw['w_kt'].shape),
            _resident((CONV_WIDTH, width)),
            _resident((1, width)),
            _resident(lw['w_gate'].shape),
            _resident(lw['b_gate'].shape),
            _resident((1, width)),
        ],
        out_specs=[tok_spec, k_spec, tok_spec, k_spec] + [tok_spec] * 4 + [state_c, state_h],
        out_shape=[tok(BF16), k_shape(F32), tok(F32), k_shape(BF16), tok(BF16), tok(BF16), tok(F32), tok(F32),
                   jax.ShapeDtypeStruct((batch, CONV_WIDTH - 1, width), F32),
                   jax.ShapeDtypeStruct((batch, 1, width), F32)],
        scratch_shapes=[
            pltpu.VMEM((nb, CONV_PAD + tt, width), F32),
            pltpu.VMEM((tm, width), F32),
            pltpu.VMEM((tm, width), F32),
            pltpu.VMEM((nb, 1, width), F32),
        ],
        compiler_params=pltpu.CompilerParams(
            dimension_semantics=("arbitrary", "arbitrary"), vmem_limit_bytes=VMEM_LIMIT),
        name="mix_in",
    )(x2d, conv_prev, h_prev, lw['g_mix'], lw['w_in'], lw['w_kt'], lw['w_conv'], lw['b_conv'],
      lw['w_gate'], lw['b_gate'], lw['lru_lambda'])


def _diff_lambda(lq1_ref, lk1_ref, lq2_ref, lk2_ref, lam_init):
    s1 = jnp.sum(lq1_ref[...] * lk1_ref[...], axis=-1, keepdims=True)
    s2 = jnp.sum(lq2_ref[...] * lk2_ref[...], axis=-1, keepdims=True)
    return jnp.exp(s1) - jnp.exp(s2) + lam_init


def _head_out(o1, o2, lam, gsub, lam_init):
    od = o1 - lam * o2
    return _rms(od, gsub) * (1.0 - lam_init)


def _prompt_attn_kernel(q_ref, kt_ref, v_ref, lq1_ref, lk1_ref, lq2_ref, lk2_ref, gsub_ref, o_ref,
                        qbd_scr, s_scr, p_scr, m_scr, l_scr, a_scr, acc_scr,
                        *, tq, qk_dim, lam_init, row_chunk):
    i = pl.program_id(2)
    tk = tq
    rows = 2 * tq
    n_hg = qbd_scr.shape[0]
    lane = lax.broadcasted_iota(jnp.int32, (tq, LANES), 1)
    for hh in range(n_hg):
        q = q_ref[:, hh * LANES:(hh + 1) * LANES]
        zero = jnp.zeros_like(q)
        qbd_scr[hh, 0:tq, :] = jnp.where(lane < qk_dim, q, zero)
        qbd_scr[hh, tq:rows, :] = jnp.where(lane >= qk_dim, q, zero)
    m_scr[...] = jnp.full(m_scr.shape, -jnp.inf, F32)
    l_scr[...] = jnp.zeros(l_scr.shape, F32)
    acc_scr[...] = jnp.zeros(acc_scr.shape, F32)

    def head_step(hh, k0, masked):
        hs = slice(hh * LANES, (hh + 1) * LANES)
        s_scr[hh] = jnp.dot(qbd_scr[hh], kt_ref[hs, pl.ds(k0, tk)], preferred_element_type=F32)
        for c in range(rows // row_chunk):
            r0 = c * row_chunk
            rs = slice(r0, r0 + row_chunk)
            s = s_scr[hh, rs, :]
            if masked:
                qpos = lax.broadcasted_iota(jnp.int32, (row_chunk, tk), 0) + (r0 % tq)
                kpos = lax.broadcasted_iota(jnp.int32, (row_chunk, tk), 1)
                s = jnp.where(kpos <= qpos, s, -jnp.inf)
            tiles = [s[:, t * LANES:(t + 1) * LANES] for t in range(tk // LANES)]
            mx = functools.reduce(jnp.maximum, tiles)
            m_old = m_scr[hh, rs, :]
            m_new = jnp.maximum(m_old, jnp.max(mx, axis=-1, keepdims=True))
            alpha = jnp.exp2(m_old - m_new)
            ps = [jnp.exp2(t - m_new) for t in tiles]
            l_scr[hh, rs, :] = alpha * l_scr[hh, rs, :] + functools.reduce(jnp.add, ps)
            m_scr[hh, rs, :] = m_new
            a_scr[hh, rs, :] = alpha
            p_scr[hh, rs, :] = jnp.concatenate(ps, axis=1).astype(BF16)
        pv = jnp.dot(p_scr[hh], v_ref[pl.ds(k0, tk), hs], preferred_element_type=F32)
        acc_scr[hh] = a_scr[hh] * acc_scr[hh] + pv

    def step(j, masked):
        k0 = pl.multiple_of(j * tk, tk)
        for hh in range(n_hg):
            head_step(hh, k0, masked)

    def unmasked(j, carry):
        step(j, False)
        return carry

    lax.fori_loop(0, i, unmasked, 0)
    step(i, True)
    lam = _diff_lambda(lq1_ref, lk1_ref, lq2_ref, lk2_ref, lam_init)
    for hh in range(n_hg):
        o = acc_scr[hh] / jnp.sum(l_scr[hh], axis=-1, keepdims=True)
        o_ref[:, hh * LANES:(hh + 1) * LANES] = _head_out(
            o[:tq], o[tq:], lam, gsub_ref[...], lam_init).astype(o_ref.dtype)


def _prompt_attention(q, kt, v, lw, *, n_heads, qk_dim, lam_init, tq, n_hg):
    batch, seq, _ = q.shape
    v_dim = v.shape[-1] // n_heads
    assert 2 * qk_dim == LANES and v_dim == LANES and n_heads % n_hg == 0
    rows = 2 * tq
    gw = n_hg * LANES
    small = lambda n: _resident((1, n))
    kern = functools.partial(_prompt_attn_kernel, tq=tq, qk_dim=qk_dim, lam_init=lam_init, row_chunk=32)
    return pl.pallas_call(
        kern,
        grid=(batch, n_heads // n_hg, seq // tq),
        in_specs=[
            pl.BlockSpec((None, tq, gw), lambda b, h, i: (b, i, h)),
            pl.BlockSpec((None, gw, seq), lambda b, h, i: (b, h, 0)),
            pl.BlockSpec((None, seq, gw), lambda b, h, i: (b, 0, h)),
            small(qk_dim), small(qk_dim), small(qk_dim), small(qk_dim), small(v_dim),
        ],
        out_specs=pl.BlockSpec((None, tq, gw), lambda b, h, i: (b, i, h)),
        out_shape=jax.ShapeDtypeStruct((batch, seq, n_heads * v_dim), BF16),
        scratch_shapes=[
            pltpu.VMEM((n_hg, rows, 2 * qk_dim), BF16),
            pltpu.VMEM((n_hg, rows, tq), F32),
            pltpu.VMEM((n_hg, rows, tq), BF16),
            pltpu.VMEM((n_hg, rows, LANES), F32),
            pltpu.VMEM((n_hg, rows, LANES), F32),
            pltpu.VMEM((n_hg, rows, LANES), F32),
            pltpu.VMEM((n_hg, rows, v_dim), F32),
        ],
        compiler_params=pltpu.CompilerParams(
            dimension_semantics=("arbitrary", "arbitrary", "arbitrary"), vmem_limit_bytes=VMEM_LIMIT),
        name="prompt_attn",
    )(q, kt, v, lw['lq1'], lw['lk1'], lw['lq2'], lw['lk2'], lw['g_subln'])


def _sample_attn_kernel(pt_ref, q_ref, kn_ref, vn_ref, *rest, n_pg, n_heads, qk_dim, page, lam_init):
    k_refs = rest[:n_pg]
    v_refs = rest[n_pg:2 * n_pg]
    lq1_ref, lk1_ref, lq2_ref, lk2_ref, gsub_ref, o_ref, qbd_ref, kbuf, vbuf, m_ref, l_ref, acc_ref = rest[2 * n_pg:]
    s_idx = pl.program_id(1)
    tq, feat = q_ref.shape
    n_maps = 2 * n_heads
    rows = n_maps * tq
    v_dim = feat // n_heads

    @pl.when(s_idx == 0)
    def _():
        qt = jnp.concatenate([q_ref[...]] * n_maps, axis=0)
        r = lax.broadcasted_iota(jnp.int32, (rows, feat), 0)
        c = lax.broadcasted_iota(jnp.int32, (rows, feat), 1)
        qbd = jnp.where(r // tq == c // qk_dim, qt, 0.0).astype(BF16)
        qbd_ref[...] = qbd
        pad = jnp.zeros((page - tq, feat), F32)
        kn = jnp.concatenate([kn_ref[...], pad], axis=0).astype(BF16)
        vn = jnp.concatenate([vn_ref[...], pad], axis=0).astype(BF16)
        s = lax.dot_general(qbd, kn, (((1,), (1,)), ((), ())), preferred_element_type=F32)
        rr = lax.broadcasted_iota(jnp.int32, (rows, page), 0)
        cc = lax.broadcasted_iota(jnp.int32, (rows, page), 1)
        s = jnp.where(cc <= rr % tq, s, -jnp.inf)
        m = jnp.max(s, axis=-1, keepdims=True)
        p = jnp.exp2(s - m)
        m_ref[...] = m
        l_ref[...] = jnp.sum(p, axis=-1, keepdims=True)
        acc_ref[...] = jnp.dot(p.astype(BF16), vn, preferred_element_type=F32)

    for g in range(n_pg):
        kbuf[:, g * page:(g + 1) * page] = k_refs[g][...].astype(BF16)
        for h in range(n_heads):
            vh = v_refs[g][pl.ds(h, page, stride=n_heads), :]
            vbuf[g * page:(g + 1) * page, h * v_dim:(h + 1) * v_dim] = vh.astype(BF16)

    s = jnp.dot(qbd_ref[...], kbuf[...], preferred_element_type=F32)
    m = m_ref[...]
    m_new = jnp.maximum(m, jnp.max(s, axis=-1, keepdims=True))
    alpha = jnp.exp2(m - m_new)
    p = jnp.exp2(s - m_new)
    l_ref[...] = alpha * l_ref[...] + jnp.sum(p, axis=-1, keepdims=True)
    acc_ref[...] = alpha * acc_ref[...] + jnp.dot(p.astype(BF16), vbuf[...], preferred_element_type=F32)
    m_ref[...] = m_new

    @pl.when(s_idx == pl.num_programs(1) - 1)
    def _():
        lam = _diff_lambda(lq1_ref, lk1_ref, lq2_ref, lk2_ref, lam_init)
        o = acc_ref[...] / l_ref[...]
        for h in range(n_heads):
            cols = slice(h * v_dim, (h + 1) * v_dim)
            o1 = o[(2 * h) * tq:(2 * h + 1) * tq, cols]
            o2 = o[(2 * h + 1) * tq:(2 * h + 2) * tq, cols]
            o_ref[:, cols] = _head_out(o1, o2, lam, gsub_ref[...], lam_init)


def _sample_attention(q, k_new, v_new, kt_pages, v_pages, page_table, lw, *, n_heads, qk_dim, lam_init, n_pg):
    dec_batch, tq, feat = q.shape
    page = kt_pages.shape[-1]
    n_pages = page_table.shape[1]
    v_dim = feat // n_heads
    rows = 2 * n_heads * tq
    assert n_pages % n_pg == 0 and rows % SUBLANES == 0 and v_dim == LANES and page == LANES
    small = lambda n: pl.BlockSpec((1, n), lambda b, s, pt: (0, 0))
    new_spec = pl.BlockSpec((None, tq, feat), lambda b, s, pt: (b, 0, 0))

    def page_spec(g, shape):
        return pl.BlockSpec((None,) + shape, lambda b, s, pt: (pt[b, s * n_pg + g], 0, 0))

    kern = functools.partial(_sample_attn_kernel, n_pg=n_pg, n_heads=n_heads, qk_dim=qk_dim, page=page,
                             lam_init=lam_init)
    grid_spec = pltpu.PrefetchScalarGridSpec(
        num_scalar_prefetch=1,
        grid=(dec_batch, n_pages // n_pg),
        in_specs=[new_spec, new_spec, new_spec]
        + [page_spec(g, (feat, page)) for g in range(n_pg)]
        + [page_spec(g, (page * n_heads, v_dim)) for g in range(n_pg)]
        + [small(qk_dim)] * 4 + [small(v_dim)],
        out_specs=new_spec,
        scratch_shapes=[
            pltpu.VMEM((rows, feat), BF16),
            pltpu.VMEM((feat, n_pg * page), BF16),
            pltpu.VMEM((n_pg * page, feat), BF16),
            pltpu.VMEM((rows, 1), F32),
            pltpu.VMEM((rows, 1), F32),
            pltpu.VMEM((rows, feat), F32),
        ],
    )
    return pl.pallas_call(
        kern,
        grid_spec=grid_spec,
        out_shape=jax.ShapeDtypeStruct((dec_batch, tq, feat), F32),
        compiler_params=pltpu.CompilerParams(
            dimension_semantics=("arbitrary", "arbitrary"), vmem_limit_bytes=VMEM_LIMIT),
        name="sample_attn",
    )(page_table, q, k_new, v_new, *([kt_pages] * n_pg), *([v_pages] * n_pg),
      lw['lq1'], lw['lk1'], lw['lq2'], lw['lk2'], lw['g_subln'])


def _mix_out_kernel(x_ref, oa_ref, rin_ref, sga_ref, sgr_ref, p_ref,
                    wattn_ref, wrec_ref, wout_ref, gffn_ref, wg_ref, wu_ref, wd_ref,
                    gple_ref, wpg_ref, wpp_ref, gfin_ref, y_ref, *, ff_chunk, final_norm):
    dot = functools.partial(jnp.dot, preferred_element_type=F32)
    o_a = dot(oa_ref[...].astype(BF16), wattn_ref[...])
    o_r = dot(rin_ref[...], wrec_ref[...])
    m = sga_ref[...] * o_a + sgr_ref[...] * o_r
    h = x_ref[...] + dot(m.astype(BF16), wout_ref[...])

    u2 = _rms(h, gffn_ref[...]).astype(BF16)
    d_ff = wg_ref.shape[1]
    ffn = jnp.zeros_like(h)
    for c0 in range(0, d_ff, ff_chunk):
        c1 = min(c0 + ff_chunk, d_ff)
        hid = jax.nn.silu(dot(u2, wg_ref[:, c0:c1])) * dot(u2, wu_ref[:, c0:c1])
        ffn = ffn + dot(hid.astype(BF16), wd_ref[c0:c1, :])
    h = h + ffn

    g = jax.nn.sigmoid(dot(_rms(h, gple_ref[...]).astype(BF16), wpg_ref[...]))
    h = h + g * dot(p_ref[...].astype(BF16), wpp_ref[...])
    y_ref[...] = _rms(h, gfin_ref[...]) if final_norm else h


def _mix_out(x2d, oa, rin, sga, sgr, p2d, lw, g_final, *, tm, final_norm):
    n_tok, d_model = x2d.shape
    tok_spec = lambda w: pl.BlockSpec((tm, w), lambda i: (i, 0))
    weights = [lw['w_attn_br'], lw['w_rec_br'], lw['w_out'], lw['g_ffn'], lw['w_ffn_gate'], lw['w_ffn_up'],
               lw['w_ffn_down'], lw['g_ple'], lw['w_ple_gate'], lw['w_ple_proj'], g_final]
    kern = functools.partial(_mix_out_kernel, ff_chunk=1024, final_norm=final_norm)
    return pl.pallas_call(
        kern,
        grid=(n_tok // tm,),
        in_specs=[tok_spec(d_model), tok_spec(oa.shape[1]), tok_spec(rin.shape[1]), tok_spec(d_model),
                  tok_spec(d_model), tok_spec(p2d.shape[1])] + [_resident(w.shape) for w in weights],
        out_specs=tok_spec(d_model),
        out_shape=jax.ShapeDtypeStruct((n_tok, d_model), F32),
        compiler_params=pltpu.CompilerParams(
            dimension_semantics=("arbitrary",), vmem_limit_bytes=VMEM_LIMIT),
        name="mix_out",
    )(x2d, oa, rin, sga, sgr, p2d, *weights)


def kernel(x_prompt, x_sample, p_prompt, p_sample, cache_k, cache_v, page_table, state_conv, state_h, g_mix, w_in, lambda_q1, lambda_k1, lambda_q2, lambda_k2, g_subln, w_attn_br, w_conv, b_conv, w_gate_a, b_gate_a, w_gate_x, b_gate_x, lru_lambda, w_rec_br, w_out, g_ffn, w_ffn_gate, w_ffn_up, w_ffn_down, g_ple, w_ple_gate, w_ple_proj, g_final):
    depth = w_in.shape[0]
    batch, seq, d_model = x_prompt.shape
    dec_batch, dec_seq, _ = x_sample.shape
    _, n_phys, page, n_heads, _, qk_dim = cache_k.shape
    v_dim = cache_v.shape[-1]
    width = w_conv.shape[-1]
    feat = n_heads * v_dim
    qk_scale = qk_dim ** -0.5 * math.log2(math.e)

    hp = x_prompt.reshape(batch * seq, d_model)
    hs = x_sample.reshape(dec_batch * dec_seq, d_model)
    outs = [[] for _ in range(8)]
    for l in range(depth):
        lam_init = 0.8 - 0.6 * math.exp(-0.3 * l)
        last = l == depth - 1
        lw = dict(
            g_mix=g_mix[l][None], w_in=w_in[l].astype(BF16),
            w_kt=w_in[l][:, feat:2 * feat].T.astype(BF16),
            lq1=lambda_q1[l][None], lk1=lambda_k1[l][None], lq2=lambda_q2[l][None], lk2=lambda_k2[l][None],
            g_subln=g_subln[l][None], w_attn_br=w_attn_br[l].astype(BF16),
            w_conv=w_conv[l], b_conv=b_conv[l][None],
            w_gate=jnp.concatenate([w_gate_a[l], w_gate_x[l]], axis=-1).astype(BF16),
            b_gate=jnp.concatenate([b_gate_a[l], b_gate_x[l]], axis=-1)[:, None, :],
            lru_lambda=lru_lambda[l][None], w_rec_br=w_rec_br[l].astype(BF16), w_out=w_out[l].astype(BF16),
            g_ffn=g_ffn[l][None], w_ffn_gate=w_ffn_gate[l].astype(BF16), w_ffn_up=w_ffn_up[l].astype(BF16),
            w_ffn_down=w_ffn_down[l].astype(BF16), g_ple=g_ple[l][None],
            w_ple_gate=w_ple_gate[l].astype(BF16), w_ple_proj=w_ple_proj[l].astype(BF16))
        gfin = g_final[None]

        conv0 = jnp.zeros((batch, CONV_WIDTH - 1, width), F32)
        h0 = jnp.zeros((batch, 1, width), F32)
        q, kf, vf, kb, vb, rin, sga, sgr, cnew, hnew = _mix_in(
            hp, conv0, h0, lw, batch=batch, seq=seq, nb=1, tt=256, qk_scale=qk_scale, k_transposed=True)
        oa = _prompt_attention(q.reshape(batch, seq, feat), kb, vb.reshape(batch, seq, feat), lw,
                               n_heads=n_heads, qk_dim=qk_dim, lam_init=lam_init, tq=256, n_hg=4)
        hp = _mix_out(hp, oa.reshape(batch * seq, feat), rin, sga, sgr,
                      p_prompt[l].reshape(batch * seq, -1), lw, gfin, tm=256, final_norm=last)
        outs[0].append(jnp.transpose(kf.reshape(batch, n_heads, 2, qk_dim, seq), (0, 4, 1, 2, 3)))
        outs[1].append(vf.reshape(batch, seq, n_heads, v_dim))
        outs[2].append(cnew)
        outs[3].append(hnew.reshape(batch, width))

        q, kf, vf, kb, vb, rin, sga, sgr, cnew, hnew = _mix_in(
            hs, state_conv[l], state_h[l][:, None, :], lw,
            batch=dec_batch, seq=dec_seq, nb=dec_batch, tt=dec_seq, qk_scale=qk_scale, k_transposed=False)
        kt_pages = jnp.transpose(cache_k[l], (0, 2, 3, 4, 1)).reshape(n_phys, feat, page)
        v_pages = cache_v[l].reshape(n_phys, page * n_heads, v_dim)
        oa = _sample_attention(q.astype(F32).reshape(dec_batch, dec_seq, feat),
                               kf.reshape(dec_batch, dec_seq, feat), vf.reshape(dec_batch, dec_seq, feat),
                               kt_pages, v_pages, page_table, lw,
                               n_heads=n_heads, qk_dim=qk_dim, lam_init=lam_init, n_pg=8)
        hs = _mix_out(hs, oa.reshape(dec_batch * dec_seq, feat), rin, sga, sgr,
                      p_sample[l].reshape(dec_batch * dec_seq, -1), lw, gfin,
                      tm=dec_batch * dec_seq, final_norm=last)
        outs[4].append(kf.reshape(dec_batch, dec_seq, n_heads, 2, qk_dim))
        outs[5].append(vf.reshape(dec_batch, dec_seq, n_heads, v_dim))
        outs[6].append(cnew)
        outs[7].append(hnew.reshape(dec_batch, width))

    y_prompt = hp.reshape(batch, seq, d_model)
    y_sample = hs.reshape(dec_batch, dec_seq, d_model)
    return (y_prompt, y_sample) + tuple(jnp.stack(o) for o in outs)
```

```python
import functools
import math

import jax
import jax.numpy as jnp
from jax import lax
from jax.experimental import pallas as pl
from jax.experimental.pallas import tpu as pltpu

F32 = jnp.float32
BF16 = jnp.bfloat16

EPS = 1e-6
LRU_C = 8.0
CONV_WIDTH = 4
N_LRU_BLOCKS = 8
SUBLANES = 8
LANES = 128
CONV_PAD = 8
PAGE_CHUNK = 8
VMEM_LIMIT = 56 * 1024 * 1024


def _rms(x, g):
    ms = jnp.mean(x * x, axis=-1, keepdims=True)
    return x * lax.rsqrt(ms + EPS) * g


def _softplus(x):
    return jnp.maximum(x, 0.0) + jnp.log1p(jnp.exp(-jnp.abs(x)))


def _resident(shape):
    nd = len(shape)
    return pl.BlockSpec(shape, lambda *_: (0,) * nd, pipeline_mode=pl.Buffered(1))


def _mix_in_kernel(x_ref, cprev_ref, hprev_ref, gmix_ref, win_ref, wkt_ref, wconv_ref, bconv_ref,
                   wgate_ref, bgate_ref, lam_ref,
                   q_ref, kf_ref, vf_ref, kb_ref, vb_ref, rin_ref, sga_ref, sgr_ref,
                   cnew_ref, hnew_ref,
                   xbuf, abuf, ubuf, hstate, *, nb, tt, width, qk_scale, k_transposed):
    t = pl.program_id(1)
    tm = nb * tt
    lru_block = width // N_LRU_BLOCKS

    @pl.when(t == 0)
    def _():
        xbuf[:, CONV_PAD - (CONV_WIDTH - 1):CONV_PAD, :] = cprev_ref[...]
        hstate[...] = hprev_ref[...]

    u = _rms(x_ref[...], gmix_ref[...]).astype(BF16)

    def proj(i):
        return jnp.dot(u, win_ref[:, i * width:(i + 1) * width], preferred_element_type=F32)

    q_ref[...] = (proj(0) * qk_scale).astype(BF16)
    if k_transposed:
        k = lax.dot_general(wkt_ref[...], u, (((1,), (1,)), ((), ())), preferred_element_type=F32)
    else:
        k = proj(1)
    kf_ref[...] = k
    kb_ref[...] = k.astype(BF16)
    v = proj(2)
    vf_ref[...] = v
    vb_ref[...] = v.astype(BF16)

    xbuf[:, CONV_PAD:CONV_PAD + tt, :] = proj(3).reshape(nb, tt, width)
    xc = bconv_ref[...].reshape(1, 1, width)
    for j in range(CONV_WIDTH):
        r0 = CONV_PAD - (CONV_WIDTH - 1) + j
        xc = xc + xbuf[:, r0:r0 + tt, :] * wconv_ref[j:j + 1, :].reshape(1, 1, width)
    tail = xbuf[:, CONV_PAD + tt - (CONV_WIDTH - 1):CONV_PAD + tt, :]
    cnew_ref[...] = tail
    xbuf[:, CONV_PAD - (CONV_WIDTH - 1):CONV_PAD, :] = tail

    xc = xc.reshape(tm, width)
    xcb = xc.astype(BF16)
    sp = _softplus(-lam_ref[...])
    for blk in range(N_LRU_BLOCKS):
        sl = slice(blk * lru_block, (blk + 1) * lru_block)
        g = jnp.dot(xcb[:, sl], wgate_ref[blk], preferred_element_type=F32) + bgate_ref[blk]
        r = jax.nn.sigmoid(g[:, :lru_block])
        ig = jax.nn.sigmoid(g[:, lru_block:])
        a = jnp.exp(-LRU_C * r * sp[:, sl])
        abuf[:, sl] = a
        ubuf[:, sl] = jnp.sqrt(1.0 - a * a) * (ig * xc[:, sl])

    n_groups = tt // SUBLANES
    row = lax.broadcasted_iota(jnp.int32, (SUBLANES, width), 0)

    for b in range(nb):
        h = hstate[b]
        for g in range(n_groups):
            r0 = (b * n_groups + g) * SUBLANES
            a = abuf[r0:r0 + SUBLANES, :]
            uu = ubuf[r0:r0 + SUBLANES, :]
            d = 1
            while d < SUBLANES:
                a_sh = jnp.where(row >= d, pltpu.roll(a, d, 0), 1.0)
                u_sh = jnp.where(row >= d, pltpu.roll(uu, d, 0), 0.0)
                uu = a * u_sh + uu
                a = a * a_sh
                d *= 2
            hb = uu + a * h
            ubuf[r0:r0 + SUBLANES, :] = hb
            h = hb[SUBLANES - 1:SUBLANES, :]
        hstate[b] = h
    hnew_ref[...] = hstate[...]

    rin_ref[...] = (ubuf[...] * jax.nn.gelu(proj(4))).astype(BF16)
    sga_ref[...] = jax.nn.sigmoid(proj(5))
    sgr_ref[...] = jax.nn.sigmoid(proj(6))


def _mix_in(x2d, conv_prev, h_prev, lw, *, batch, seq, nb, tt, qk_scale, k_transposed):
    n_tok, d_model = x2d.shape
    width = lw['w_conv'].shape[-1]
    n_t = seq // tt
    tm = nb * tt
    grid = (batch // nb, n_t)
    tok = lambda dt: jax.ShapeDtypeStruct((n_tok, width), dt)
    tok_spec = pl.BlockSpec((tm, width), lambda b, t: (b * n_t + t, 0))
    if k_transposed:
        assert nb == 1
        k_shape = lambda dt: jax.ShapeDtypeStruct((batch, width, seq), dt)
        k_spec = pl.BlockSpec((None, width, tt), lambda b, t: (b, 0, t))
    else:
        k_shape, k_spec = tok, tok_spec
    state_c = pl.BlockSpec((nb, CONV_WIDTH - 1, width), lambda b, t: (b, 0, 0))
    state_h = pl.BlockSpec((nb, 1, width), lambda b, t: (b, 0, 0))
    kern = functools.partial(_mix_in_kernel, nb=nb, tt=tt, width=width, qk_scale=qk_scale,
                             k_transposed=k_transposed)
    return pl.pallas_call(
        kern,
        grid=grid,
        in_specs=[
            pl.BlockSpec((tm, d_model), lambda b, t: (b * n_t + t, 0)),
            state_c, state_h,
            _resident((1, d_model)),
            _resident(lw['w_in'].shape),
            _resident(lw['w_kt'].shape),
            _resident((CONV_WIDTH, width)),
            _resident((1, width)),
            _resident(lw['w_gate'].shape),
            _resident(lw['b_gate'].shape),
            _resident((1, width)),
        ],
        out_specs=[tok_spec, k_spec, tok_spec, k_spec] + [tok_spec] * 4 + [state_c, state_h],
        out_shape=[tok(BF16), k_shape(F32), tok(F32), k_shape(BF16), tok(BF16), tok(BF16), tok(F32), tok(F32),
                   jax.ShapeDtypeStruct((batch, CONV_WIDTH - 1, width), F32),
                   jax.ShapeDtypeStruct((batch, 1, width), F32)],
        scratch_shapes=[
            pltpu.VMEM((nb, CONV_PAD + tt, width), F32),
            pltpu.VMEM((tm, width), F32),
            pltpu.VMEM((tm, width), F32),
            pltpu.VMEM((nb, 1, width), F32),
        ],
        compiler_params=pltpu.CompilerParams(
            dimension_semantics=("arbitrary", "arbitrary"), vmem_limit_bytes=VMEM_LIMIT),
        name="mix_in",
    )(x2d, conv_prev, h_prev, lw['g_mix'], lw['w_in'], lw['w_kt'], lw['w_conv'], lw['b_conv'],
      lw['w_gate'], lw['b_gate'], lw['lru_lambda'])


def _diff_lambda(lq1_ref, lk1_ref, lq2_ref, lk2_ref, lam_init):
    s1 = jnp.sum(lq1_ref[...] * lk1_ref[...], axis=-1, keepdims=True)
    s2 = jnp.sum(lq2_ref[...] * lk2_ref[...], axis=-1, keepdims=True)
    return jnp.exp(s1) - jnp.exp(s2) + lam_init


def _head_out(o1, o2, lam, gsub, lam_init):
    od = o1 - lam * o2
    return _rms(od, gsub) * (1.0 - lam_init)


def _prompt_attn_body(i, q_ref, kt_ref, v_ref, lam, gsub_ref, o_ref,
                      qbd_scr, s_scr, p_scr, m_scr, l_scr, a_scr, acc_scr, *, tq, qk_dim, lam_init, row_chunk):
    tk = tq
    rows = 2 * tq
    n_hg = qbd_scr.shape[0]
    lane = lax.broadcasted_iota(jnp.int32, (tq, LANES), 1)
    for hh in range(n_hg):
        q = q_ref[:, hh * LANES:(hh + 1) * LANES]
        zero = jnp.zeros_like(q)
        qbd_scr[hh, 0:tq, :] = jnp.where(lane < qk_dim, q, zero)
        qbd_scr[hh, tq:rows, :] = jnp.where(lane >= qk_dim, q, zero)
    m_scr[...] = jnp.full(m_scr.shape, -jnp.inf, F32)
    l_scr[...] = jnp.zeros(l_scr.shape, F32)
    acc_scr[...] = jnp.zeros(acc_scr.shape, F32)

    def head_step(hh, k0, masked):
        hs = slice(hh * LANES, (hh + 1) * LANES)
        s_scr[hh] = jnp.dot(qbd_scr[hh], kt_ref[hs, pl.ds(k0, tk)], preferred_element_type=F32)
        for c in range(rows // row_chunk):
            r0 = c * row_chunk
            rs = slice(r0, r0 + row_chunk)
            s = s_scr[hh, rs, :]
            if masked:
                qpos = lax.broadcasted_iota(jnp.int32, (row_chunk, tk), 0) + (r0 % tq)
                kpos = lax.broadcasted_iota(jnp.int32, (row_chunk, tk), 1)
                s = jnp.where(kpos <= qpos, s, -jnp.inf)
            tiles = [s[:, t * LANES:(t + 1) * LANES] for t in range(tk // LANES)]
            mx = functools.reduce(jnp.maximum, tiles)
            m_old = m_scr[hh, rs, :]
            m_new = jnp.maximum(m_old, jnp.max(mx, axis=-1, keepdims=True))
            alpha = jnp.exp2(m_old - m_new)
            ps = [jnp.exp2(t - m_new) for t in tiles]
            l_scr[hh, rs, :] = alpha * l_scr[hh, rs, :] + functools.reduce(jnp.add, ps)
            m_scr[hh, rs, :] = m_new
            a_scr[hh, rs, :] = alpha
            p_scr[hh, rs, :] = jnp.concatenate(ps, axis=1).astype(BF16)
        pv = jnp.dot(p_scr[hh], v_ref[pl.ds(k0, tk), hs], preferred_element_type=F32)
        acc_scr[hh] = a_scr[hh] * acc_scr[hh] + pv

    def step(j, masked):
        k0 = pl.multiple_of(j * tk, tk)
        for hh in range(n_hg):
            head_step(hh, k0, masked)

    def unmasked(j, carry):
        step(j, False)
        return carry

    lax.fori_loop(0, i, unmasked, 0)
    step(i, True)
    for hh in range(n_hg):
        o = acc_scr[hh] / jnp.sum(l_scr[hh], axis=-1, keepdims=True)
        o_ref[:, hh * LANES:(hh + 1) * LANES] = _head_out(
            o[:tq], o[tq:], lam, gsub_ref[...], lam_init).astype(o_ref.dtype)


def _sample_attn_body(s_idx, n_steps, q_ref, kn_ref, vn_ref, k_refs, v_refs, lam, gsub_ref, o_ref,
                      qbd_ref, kbuf, vbuf, m_ref, l_ref, acc_ref, *, n_heads, qk_dim, page, lam_init):
    tq, feat = q_ref.shape
    n_maps = 2 * n_heads
    rows = n_maps * tq
    v_dim = feat // n_heads

    @pl.when(s_idx == 0)
    def _():
        qt = jnp.concatenate([q_ref[...]] * n_maps, axis=0)
        r = lax.broadcasted_iota(jnp.int32, (rows, feat), 0)
        c = lax.broadcasted_iota(jnp.int32, (rows, feat), 1)
        qbd = jnp.where(r // tq == c // qk_dim, qt, 0.0).astype(BF16)
        qbd_ref[...] = qbd
        pad = jnp.zeros((page - tq, feat), F32)
        kn = jnp.concatenate([kn_ref[...], pad], axis=0).astype(BF16)
        vn = jnp.concatenate([vn_ref[...], pad], axis=0).astype(BF16)
        s = lax.dot_general(qbd, kn, (((1,), (1,)), ((), ())), preferred_element_type=F32)
        rr = lax.broadcasted_iota(jnp.int32, (rows, page), 0)
        cc = lax.broadcasted_iota(jnp.int32, (rows, page), 1)
        s = jnp.where(cc <= rr % tq, s, -jnp.inf)
        m = jnp.max(s, axis=-1, keepdims=True)
        p = jnp.exp2(s - m)
        m_ref[...] = m
        l_ref[...] = jnp.sum(p, axis=-1, keepdims=True)
        acc_ref[...] = jnp.dot(p.astype(BF16), vn, preferred_element_type=F32)

    for c0 in range(0, len(k_refs), PAGE_CHUNK):
        for g in range(PAGE_CHUNK):
            kbuf[:, g * page:(g + 1) * page] = k_refs[c0 + g][...].astype(BF16)
            for h in range(n_heads):
                vh = v_refs[c0 + g][pl.ds(h, page, stride=n_heads), :]
                vbuf[g * page:(g + 1) * page, h * v_dim:(h + 1) * v_dim] = vh.astype(BF16)
        s = jnp.dot(qbd_ref[...], kbuf[...], preferred_element_type=F32)
        m = m_ref[...]
        m_new = jnp.maximum(m, jnp.max(s, axis=-1, keepdims=True))
        alpha = jnp.exp2(m - m_new)
        p = jnp.exp2(s - m_new)
        l_ref[...] = alpha * l_ref[...] + jnp.sum(p, axis=-1, keepdims=True)
        acc_ref[...] = alpha * acc_ref[...] + jnp.dot(p.astype(BF16), vbuf[...], preferred_element_type=F32)
        m_ref[...] = m_new

    @pl.when(s_idx == n_steps - 1)
    def _():
        o = acc_ref[...] / l_ref[...]
        for h in range(n_heads):
            cols = slice(h * v_dim, (h + 1) * v_dim)
            o1 = o[(2 * h) * tq:(2 * h + 1) * tq, cols]
            o2 = o[(2 * h + 1) * tq:(2 * h + 2) * tq, cols]
            o_ref[:, cols] = _head_out(o1, o2, lam, gsub_ref[...], lam_init)


def _attn_kernel(pt_ref, q_ref, kt_ref, v_ref, qs_ref, kn_ref, vn_ref, *rest,
                 n_pg, steps_per_sample, tq, n_heads, qk_dim, page, lam_init):
    del pt_ref
    k_refs = rest[:n_pg]
    v_refs = rest[n_pg:2 * n_pg]
    (lq1_ref, lk1_ref, lq2_ref, lk2_ref, gsub_ref, o_ref, os_ref,
     qbd_scr, s_scr, p_scr, m_scr, l_scr, a_scr, acc_scr,
     sqbd_scr, kbuf, vbuf, sm_scr, sl_scr, sacc_scr) = rest[2 * n_pg:]
    i = pl.program_id(2)
    step = (pl.program_id(0) * pl.num_programs(1) + pl.program_id(1)) * pl.num_programs(2) + i
    lam = _diff_lambda(lq1_ref, lk1_ref, lq2_ref, lk2_ref, lam_init)
    _sample_attn_body(step % steps_per_sample, steps_per_sample, qs_ref, kn_ref, vn_ref, k_refs, v_refs,
                      lam, gsub_ref, os_ref, sqbd_scr, kbuf, vbuf, sm_scr, sl_scr, sacc_scr,
                      n_heads=n_heads, qk_dim=qk_dim, page=page, lam_init=lam_init)
    _prompt_attn_body(i, q_ref, kt_ref, v_ref, lam, gsub_ref, o_ref,
                      qbd_scr, s_scr, p_scr, m_scr, l_scr, a_scr, acc_scr,
                      tq=tq, qk_dim=qk_dim, lam_init=lam_init, row_chunk=32)


def _attention(q, kt, v, qs, k_new, v_new, kt_pages, v_pages, page_table, lw, *,
               n_heads, qk_dim, lam_init, tq, n_hg):
    batch, seq, feat = q.shape
    dec_batch, dec_seq, _ = qs.shape
    page = kt_pages.shape[-1]
    n_pages = page_table.shape[1]
    v_dim = feat // n_heads
    n_groups = n_heads // n_hg
    n_q = seq // tq
    total_steps = batch * n_groups * n_q
    assert 2 * qk_dim == LANES and v_dim == LANES and page == LANES and n_heads % n_hg == 0
    assert (dec_batch * n_pages) % total_steps == 0
    n_pg = dec_batch * n_pages // total_steps
    assert n_pages % n_pg == 0 and n_pg % PAGE_CHUNK == 0
    steps_per_sample = n_pages // n_pg
    rows = 2 * tq
    srows = 2 * n_heads * dec_seq
    gw = n_hg * LANES

    def linear(b, h, i):
        return (b * n_groups + h) * n_q + i

    small = lambda n: pl.BlockSpec((1, n), lambda b, h, i, pt: (0, 0))
    new_spec = pl.BlockSpec((None, dec_seq, feat), lambda b, h, i, pt: (linear(b, h, i) // steps_per_sample, 0, 0))

    def page_spec(g, shape):
        def index(b, h, i, pt):
            n = linear(b, h, i)
            return (pt[n // steps_per_sample, (n % steps_per_sample) * n_pg + g], 0, 0)
        return pl.BlockSpec((None,) + shape, index)

    kern = functools.partial(_attn_kernel, n_pg=n_pg, steps_per_sample=steps_per_sample, tq=tq,
                             n_heads=n_heads, qk_dim=qk_dim, page=page, lam_init=lam_init)
    grid_spec = pltpu.PrefetchScalarGridSpec(
        num_scalar_prefetch=1,
        grid=(batch, n_groups, n_q),
        in_specs=[
            pl.BlockSpec((None, tq, gw), lambda b, h, i, pt: (b, i, h)),
            pl.BlockSpec((None, gw, seq), lambda b, h, i, pt: (b, h, 0), pipeline_mode=pl.Buffered(1)),
            pl.BlockSpec((None, seq, gw), lambda b, h, i, pt: (b, 0, h), pipeline_mode=pl.Buffered(1)),
            new_spec, new_spec, new_spec]
        + [page_spec(g, (feat, page)) for g in range(n_pg)]
        + [page_spec(g, (page * n_heads, v_dim)) for g in range(n_pg)]
        + [small(qk_dim)] * 4 + [small(v_dim)],
        out_specs=[pl.BlockSpec((None, tq, gw), lambda b, h, i, pt: (b, i, h)), new_spec],
        scratch_shapes=[
            pltpu.VMEM((n_hg, rows, 2 * qk_dim), BF16),
            pltpu.VMEM((n_hg, rows, tq), F32),
            pltpu.VMEM((n_hg, rows, tq), BF16),
            pltpu.VMEM((n_hg, rows, LANES), F32),
            pltpu.VMEM((n_hg, rows, LANES), F32),
            pltpu.VMEM((n_hg, rows, LANES), F32),
            pltpu.VMEM((n_hg, rows, v_dim), F32),
            pltpu.VMEM((srows, feat), BF16),
            pltpu.VMEM((feat, PAGE_CHUNK * page), BF16),
            pltpu.VMEM((PAGE_CHUNK * page, feat), BF16),
            pltpu.VMEM((srows, 1), F32),
            pltpu.VMEM((srows, 1), F32),
            pltpu.VMEM((srows, feat), F32),
        ],
    )
    return pl.pallas_call(
        kern,
        grid_spec=grid_spec,
        out_shape=[jax.ShapeDtypeStruct((batch, seq, feat), BF16),
                   jax.ShapeDtypeStruct((dec_batch, dec_seq, feat), F32)],
        compiler_params=pltpu.CompilerParams(
            dimension_semantics=("arbitrary", "arbitrary", "arbitrary"), vmem_limit_bytes=VMEM_LIMIT),
        name="attention",
    )(page_table, q, kt, v, qs, k_new, v_new, *([kt_pages] * n_pg), *([v_pages] * n_pg),
      lw['lq1'], lw['lk1'], lw['lq2'], lw['lk2'], lw['g_subln'])


def _mix_out_kernel(x_ref, oa_ref, rin_ref, sga_ref, sgr_ref, p_ref,
                    wattn_ref, wrec_ref, wout_ref, gffn_ref, wg_ref, wu_ref, wd_ref,
                    gple_ref, wpg_ref, wpp_ref, gfin_ref, y_ref, *, ff_chunk, final_norm):
    dot = functools.partial(jnp.dot, preferred_element_type=F32)
    o_a = dot(oa_ref[...].astype(BF16), wattn_ref[...])
    o_r = dot(rin_ref[...], wrec_ref[...])
    m = sga_ref[...] * o_a + sgr_ref[...] * o_r
    h = x_ref[...] + dot(m.astype(BF16), wout_ref[...])

    u2 = _rms(h, gffn_ref[...]).astype(BF16)
    d_ff = wg_ref.shape[1]
    ffn = jnp.zeros_like(h)
    for c0 in range(0, d_ff, ff_chunk):
        c1 = min(c0 + ff_chunk, d_ff)
        hid = jax.nn.silu(dot(u2, wg_ref[:, c0:c1])) * dot(u2, wu_ref[:, c0:c1])
        ffn = ffn + dot(hid.astype(BF16), wd_ref[c0:c1, :])
    h = h + ffn

    g = jax.nn.sigmoid(dot(_rms(h, gple_ref[...]).astype(BF16), wpg_ref[...]))
    h = h + g * dot(p_ref[...].astype(BF16), wpp_ref[...])
    y_ref[...] = _rms(h, gfin_ref[...]) if final_norm else h


def _mix_out(x2d, oa, rin, sga, sgr, p2d, lw, g_final, *, tm, final_norm):
    n_tok, d_model = x2d.shape
    tok_spec = lambda w: pl.BlockSpec((tm, w), lambda i: (i, 0))
    weights = [lw['w_attn_br'], lw['w_rec_br'], lw['w_out'], lw['g_ffn'], lw['w_ffn_gate'], lw['w_ffn_up'],
               lw['w_ffn_down'], lw['g_ple'], lw['w_ple_gate'], lw['w_ple_proj'], g_final]
    kern = functools.partial(_mix_out_kernel, ff_chunk=1024, final_norm=final_norm)
    return pl.pallas_call(
        kern,
        grid=(n_tok // tm,),
        in_specs=[tok_spec(d_model), tok_spec(oa.shape[1]), tok_spec(rin.shape[1]), tok_spec(d_model),
                  tok_spec(d_model), tok_spec(p2d.shape[1])] + [_resident(w.shape) for w in weights],
        out_specs=tok_spec(d_model),
        out_shape=jax.ShapeDtypeStruct((n_tok, d_model), F32),
        compiler_params=pltpu.CompilerParams(
            dimension_semantics=("arbitrary",), vmem_limit_bytes=VMEM_LIMIT),
        name="mix_out",
    )(x2d, oa, rin, sga, sgr, p2d, *weights)


def kernel(x_prompt, x_sample, p_prompt, p_sample, cache_k, cache_v, page_table, state_conv, state_h, g_mix, w_in, lambda_q1, lambda_k1, lambda_q2, lambda_k2, g_subln, w_attn_br, w_conv, b_conv, w_gate_a, b_gate_a, w_gate_x, b_gate_x, lru_lambda, w_rec_br, w_out, g_ffn, w_ffn_gate, w_ffn_up, w_ffn_down, g_ple, w_ple_gate, w_ple_proj, g_final):
    depth = w_in.shape[0]
    batch, seq, d_model = x_prompt.shape
    dec_batch, dec_seq, _ = x_sample.shape
    _, n_phys, page, n_heads, _, qk_dim = cache_k.shape
    v_dim = cache_v.shape[-1]
    width = w_conv.shape[-1]
    feat = n_heads * v_dim
    qk_scale = qk_dim ** -0.5 * math.log2(math.e)

    hp = x_prompt.reshape(batch * seq, d_model)
    hs = x_sample.reshape(dec_batch * dec_seq, d_model)
    outs = [[] for _ in range(8)]
    for l in range(depth):
        lam_init = 0.8 - 0.6 * math.exp(-0.3 * l)
        last = l == depth - 1
        lw = dict(
            g_mix=g_mix[l][None], w_in=w_in[l].astype(BF16),
            w_kt=w_in[l][:, feat:2 * feat].T.astype(BF16),
            lq1=lambda_q1[l][None], lk1=lambda_k1[l][None], lq2=lambda_q2[l][None], lk2=lambda_k2[l][None],
            g_subln=g_subln[l][None], w_attn_br=w_attn_br[l].astype(BF16),
            w_conv=w_conv[l], b_conv=b_conv[l][None],
            w_gate=jnp.concatenate([w_gate_a[l], w_gate_x[l]], axis=-1).astype(BF16),
            b_gate=jnp.concatenate([b_gate_a[l], b_gate_x[l]], axis=-1)[:, None, :],
            lru_lambda=lru_lambda[l][None], w_rec_br=w_rec_br[l].astype(BF16), w_out=w_out[l].astype(BF16),
            g_ffn=g_ffn[l][None], w_ffn_gate=w_ffn_gate[l].astype(BF16), w_ffn_up=w_ffn_up[l].astype(BF16),
            w_ffn_down=w_ffn_down[l].astype(BF16), g_ple=g_ple[l][None],
            w_ple_gate=w_ple_gate[l].astype(BF16), w_ple_proj=w_ple_proj[l].astype(BF16))
        gfin = g_final[None]

        conv0 = jnp.zeros((batch, CONV_WIDTH - 1, width), F32)
        h0 = jnp.zeros((batch, 1, width), F32)
        q, kf, vf, kb, vb, rin, sga, sgr, cnew, hnew = _mix_in(
            hp, conv0, h0, lw, batch=batch, seq=seq, nb=1, tt=256, qk_scale=qk_scale, k_transposed=True)
        qs, kfs, vfs, _, _, rins, sgas, sgrs, cnews, hnews = _mix_in(
            hs, state_conv[l], state_h[l][:, None, :], lw,
            batch=dec_batch, seq=dec_seq, nb=dec_batch, tt=dec_seq, qk_scale=qk_scale, k_transposed=False)

        kt_pages = jnp.transpose(cache_k[l], (0, 2, 3, 4, 1)).reshape(n_phys, feat, page)
        v_pages = cache_v[l].reshape(n_phys, page * n_heads, v_dim)
        oa, oas = _attention(
            q.reshape(batch, seq, feat), kb, vb.reshape(batch, seq, feat),
            qs.astype(F32).reshape(dec_batch, dec_seq, feat),
            kfs.reshape(dec_batch, dec_seq, feat), vfs.reshape(dec_batch, dec_seq, feat),
            kt_pages, v_pages, page_table, lw,
            n_heads=n_heads, qk_dim=qk_dim, lam_init=lam_init, tq=256, n_hg=4)

        hp = _mix_out(hp, oa.reshape(batch * seq, feat), rin, sga, sgr,
                      p_prompt[l].reshape(batch * seq, -1), lw, gfin, tm=256, final_norm=last)
        hs = _mix_out(hs, oas.reshape(dec_batch * dec_seq, feat), rins, sgas, sgrs,
                      p_sample[l].reshape(dec_batch * dec_seq, -1), lw, gfin,
                      tm=dec_batch * dec_seq, final_norm=last)

        outs[0].append(jnp.transpose(kf.reshape(batch, n_heads, 2, qk_dim, seq), (0, 4, 1, 2, 3)))
        outs[1].append(vf.reshape(batch, seq, n_heads, v_dim))
        outs[2].append(cnew)
        outs[3].append(hnew.reshape(batch, width))
        outs[4].append(kfs.reshape(dec_batch, dec_seq, n_heads, 2, qk_dim))
        outs[5].append(vfs.reshape(dec_batch, dec_seq, n_heads, v_dim))
        outs[6].append(cnews)
        outs[7].append(hnews.reshape(dec_batch, width))

    y_prompt = hp.reshape(batch, seq, d_model)
    y_sample = hs.reshape(dec_batch, dec_seq, d_model)
    return (y_prompt, y_sample) + tuple(jnp.stack(o) for o in outs)
```

```python
import functools
import math

import jax
import jax.numpy as jnp
from jax import lax
from jax.experimental import pallas as pl
from jax.experimental.pallas import tpu as pltpu

F32 = jnp.float32
BF16 = jnp.bfloat16

EPS = 1e-6
LRU_C = 8.0
CONV_WIDTH = 4
N_LRU_BLOCKS = 8
SUBLANES = 8
LANES = 128
CONV_PAD = 8
PAGE_CHUNK = 8
VMEM_LIMIT = 56 * 1024 * 1024


def _rms(x, g):
    ms = jnp.mean(x * x, axis=-1, keepdims=True)
    return x * lax.rsqrt(ms + EPS) * g


def _softplus(x):
    return jnp.maximum(x, 0.0) + jnp.log1p(jnp.exp(-jnp.abs(x)))


def _resident(shape):
    nd = len(shape)
    return pl.BlockSpec(shape, lambda *_: (0,) * nd, pipeline_mode=pl.Buffered(1))


def _mix_in_kernel(x_ref, cprev_ref, hprev_ref, gmix_ref, win_ref, wkt_ref, wconv_ref, bconv_ref,
                   wgate_ref, bgate_ref, lam_ref,
                   q_ref, kf_ref, vf_ref, kb_ref, vb_ref, rin_ref, sga_ref, sgr_ref,
                   cnew_ref, hnew_ref,
                   xbuf, abuf, ubuf, hstate, *, nb, tt, width, qk_scale, k_transposed):
    t = pl.program_id(1)
    tm = nb * tt
    lru_block = width // N_LRU_BLOCKS

    @pl.when(t == 0)
    def _():
        xbuf[:, CONV_PAD - (CONV_WIDTH - 1):CONV_PAD, :] = cprev_ref[...]
        hstate[...] = hprev_ref[...]

    u = _rms(x_ref[...], gmix_ref[...]).astype(BF16)

    def proj(i):
        return jnp.dot(u, win_ref[:, i * width:(i + 1) * width], preferred_element_type=F32)

    q_ref[...] = (proj(0) * qk_scale).astype(BF16)
    if k_transposed:
        k = lax.dot_general(wkt_ref[...], u, (((1,), (1,)), ((), ())), preferred_element_type=F32)
    else:
        k = proj(1)
    kf_ref[...] = k
    kb_ref[...] = k.astype(BF16)
    v = proj(2)
    vf_ref[...] = v
    vb_ref[...] = v.astype(BF16)

    xbuf[:, CONV_PAD:CONV_PAD + tt, :] = proj(3).reshape(nb, tt, width)
    xc = bconv_ref[...].reshape(1, 1, width)
    for j in range(CONV_WIDTH):
        r0 = CONV_PAD - (CONV_WIDTH - 1) + j
        xc = xc + xbuf[:, r0:r0 + tt, :] * wconv_ref[j:j + 1, :].reshape(1, 1, width)
    tail = xbuf[:, CONV_PAD + tt - (CONV_WIDTH - 1):CONV_PAD + tt, :]
    cnew_ref[...] = tail
    xbuf[:, CONV_PAD - (CONV_WIDTH - 1):CONV_PAD, :] = tail

    xc = xc.reshape(tm, width)
    xcb = xc.astype(BF16)
    sp = _softplus(-lam_ref[...])
    for blk in range(N_LRU_BLOCKS):
        sl = slice(blk * lru_block, (blk + 1) * lru_block)
        g = jnp.dot(xcb[:, sl], wgate_ref[blk], preferred_element_type=F32) + bgate_ref[blk]
        r = jax.nn.sigmoid(g[:, :lru_block])
        ig = jax.nn.sigmoid(g[:, lru_block:])
        a = jnp.exp(-LRU_C * r * sp[:, sl])
        abuf[:, sl] = a
        ubuf[:, sl] = jnp.sqrt(1.0 - a * a) * (ig * xc[:, sl])

    n_groups = tt // SUBLANES
    row = lax.broadcasted_iota(jnp.int32, (SUBLANES, width), 0)

    for b in range(nb):
        h = hstate[b]
        for g in range(n_groups):
            r0 = (b * n_groups + g) * SUBLANES
            a = abuf[r0:r0 + SUBLANES, :]
            uu = ubuf[r0:r0 + SUBLANES, :]
            d = 1
            while d < SUBLANES:
                a_sh = jnp.where(row >= d, pltpu.roll(a, d, 0), 1.0)
                u_sh = jnp.where(row >= d, pltpu.roll(uu, d, 0), 0.0)
                uu = a * u_sh + uu
                a = a * a_sh
                d *= 2
            hb = uu + a * h
            ubuf[r0:r0 + SUBLANES, :] = hb
            h = hb[SUBLANES - 1:SUBLANES, :]
        hstate[b] = h
    hnew_ref[...] = hstate[...]

    rin_ref[...] = (ubuf[...] * jax.nn.gelu(proj(4))).astype(BF16)
    sga_ref[...] = jax.nn.sigmoid(proj(5))
    sgr_ref[...] = jax.nn.sigmoid(proj(6))


def _mix_in(x2d, conv_prev, h_prev, lw, *, batch, seq, nb, tt, qk_scale, k_transposed):
    n_tok, d_model = x2d.shape
    width = lw['w_conv'].shape[-1]
    n_t = seq // tt
    tm = nb * tt
    grid = (batch // nb, n_t)
    tok = lambda dt: jax.ShapeDtypeStruct((n_tok, width), dt)
    tok_spec = pl.BlockSpec((tm, width), lambda b, t: (b * n_t + t, 0))
    if k_transposed:
        assert nb == 1
        k_shape = lambda dt: jax.ShapeDtypeStruct((batch, width, seq), dt)
        k_spec = pl.BlockSpec((None, width, tt), lambda b, t: (b, 0, t))
    else:
        k_shape, k_spec = tok, tok_spec
    state_c = pl.BlockSpec((nb, CONV_WIDTH - 1, width), lambda b, t: (b, 0, 0))
    state_h = pl.BlockSpec((nb, 1, width), lambda b, t: (b, 0, 0))
    kern = functools.partial(_mix_in_kernel, nb=nb, tt=tt, width=width, qk_scale=qk_scale,
                             k_transposed=k_transposed)
    return pl.pallas_call(
        kern,
        grid=grid,
        in_specs=[
            pl.BlockSpec((tm, d_model), lambda b, t: (b * n_t + t, 0)),
            state_c, state_h,
            _resident((1, d_model)),
            _resident(lw['w_in'].shape),
            _resident(lw['w_kt'].shape),
            _resident((CONV_WIDTH, width)),
            _resident((1, width)),
            _resident(lw['w_gate'].shape),
            _resident(lw['b_gate'].shape),
            _resident((1, width)),
        ],
        out_specs=[tok_spec, k_spec, tok_spec, k_spec] + [tok_spec] * 4 + [state_c, state_h],
        out_shape=[tok(BF16), k_shape(F32), tok(F32), k_shape(BF16), tok(BF16), tok(BF16), tok(F32), tok(F32),
                   jax.ShapeDtypeStruct((batch, CONV_WIDTH - 1, width), F32),
                   jax.ShapeDtypeStruct((batch, 1, width), F32)],
        scratch_shapes=[
            pltpu.VMEM((nb, CONV_PAD + tt, width), F32),
            pltpu.VMEM((tm, width), F32),
            pltpu.VMEM((tm, width), F32),
            pltpu.VMEM((nb, 1, width), F32),
        ],
        compiler_params=pltpu.CompilerParams(
            dimension_semantics=("arbitrary", "arbitrary"), vmem_limit_bytes=VMEM_LIMIT),
        name="mix_in",
    )(x2d, conv_prev, h_prev, lw['g_mix'], lw['w_in'], lw['w_kt'], lw['w_conv'], lw['b_conv'],
      lw['w_gate'], lw['b_gate'], lw['lru_lambda'])


def _diff_lambda(lq1_ref, lk1_ref, lq2_ref, lk2_ref, lam_init):
    s1 = jnp.sum(lq1_ref[...] * lk1_ref[...], axis=-1, keepdims=True)
    s2 = jnp.sum(lq2_ref[...] * lk2_ref[...], axis=-1, keepdims=True)
    return jnp.exp(s1) - jnp.exp(s2) + lam_init


def _head_out(o1, o2, lam, gsub, lam_init):
    od = o1 - lam * o2
    return _rms(od, gsub) * (1.0 - lam_init)


def _prompt_attn_body(i, q_ref, kt_ref, v_ref, lam, gsub_ref, o_ref,
                      qbd_scr, s_scr, p_scr, m_scr, l_scr, a_scr, acc_scr, *, tq, qk_dim, lam_init, row_chunk):
    tk = tq
    rows = 2 * tq
    n_hg = qbd_scr.shape[0]
    lane = lax.broadcasted_iota(jnp.int32, (tq, LANES), 1)
    for hh in range(n_hg):
        q = q_ref[:, hh * LANES:(hh + 1) * LANES]
        zero = jnp.zeros_like(q)
        qbd_scr[hh, 0:tq, :] = jnp.where(lane < qk_dim, q, zero)
        qbd_scr[hh, tq:rows, :] = jnp.where(lane >= qk_dim, q, zero)
    m_scr[...] = jnp.full(m_scr.shape, -jnp.inf, F32)
    l_scr[...] = jnp.zeros(l_scr.shape, F32)
    acc_scr[...] = jnp.zeros(acc_scr.shape, F32)

    def head_step(hh, k0, masked):
        hs = slice(hh * LANES, (hh + 1) * LANES)
        s_scr[hh] = jnp.dot(qbd_scr[hh], kt_ref[hs, pl.ds(k0, tk)], preferred_element_type=F32)
        for c in range(rows // row_chunk):
            r0 = c * row_chunk
            rs = slice(r0, r0 + row_chunk)
            s = s_scr[hh, rs, :]
            if masked:
                qpos = lax.broadcasted_iota(jnp.int32, (row_chunk, tk), 0) + (r0 % tq)
                kpos = lax.broadcasted_iota(jnp.int32, (row_chunk, tk), 1)
                s = jnp.where(kpos <= qpos, s, -jnp.inf)
            tiles = [s[:, t * LANES:(t + 1) * LANES] for t in range(tk // LANES)]
            mx = functools.reduce(jnp.maximum, tiles)
            m_old = m_scr[hh, rs, :]
            m_new = jnp.maximum(m_old, jnp.max(mx, axis=-1, keepdims=True))
            alpha = jnp.exp2(m_old - m_new)
            ps = [jnp.exp2(t - m_new) for t in tiles]
            l_scr[hh, rs, :] = alpha * l_scr[hh, rs, :] + functools.reduce(jnp.add, ps)
            m_scr[hh, rs, :] = m_new
            a_scr[hh, rs, :] = alpha
            p_scr[hh, rs, :] = jnp.concatenate(ps, axis=1).astype(BF16)
        pv = jnp.dot(p_scr[hh], v_ref[pl.ds(k0, tk), hs], preferred_element_type=F32)
        acc_scr[hh] = a_scr[hh] * acc_scr[hh] + pv

    def step(j, masked):
        k0 = pl.multiple_of(j * tk, tk)
        for hh in range(n_hg):
            head_step(hh, k0, masked)

    def unmasked(j, carry):
        step(j, False)
        return carry

    lax.fori_loop(0, i, unmasked, 0)
    step(i, True)
    for hh in range(n_hg):
        o = acc_scr[hh] / jnp.sum(l_scr[hh], axis=-1, keepdims=True)
        o_ref[:, hh * LANES:(hh + 1) * LANES] = _head_out(
            o[:tq], o[tq:], lam, gsub_ref[...], lam_init).astype(o_ref.dtype)


def _sample_attn_body(s_idx, n_steps, q_ref, kn_ref, vn_ref, k_refs, v_refs, lam, gsub_ref, o_ref,
                      qbd_ref, kbuf, vbuf, m_ref, l_ref, acc_ref, *, n_heads, qk_dim, page, lam_init):
    tq, feat = q_ref.shape
    n_maps = 2 * n_heads
    rows = n_maps * tq
    v_dim = feat // n_heads

    @pl.when(s_idx == 0)
    def _():
        qt = jnp.concatenate([q_ref[...]] * n_maps, axis=0)
        r = lax.broadcasted_iota(jnp.int32, (rows, feat), 0)
        c = lax.broadcasted_iota(jnp.int32, (rows, feat), 1)
        qbd = jnp.where(r // tq == c // qk_dim, qt, 0.0).astype(BF16)
        qbd_ref[...] = qbd
        pad = jnp.zeros((page - tq, feat), F32)
        kn = jnp.concatenate([kn_ref[...], pad], axis=0).astype(BF16)
        vn = jnp.concatenate([vn_ref[...], pad], axis=0).astype(BF16)
        s = lax.dot_general(qbd, kn, (((1,), (1,)), ((), ())), preferred_element_type=F32)
        rr = lax.broadcasted_iota(jnp.int32, (rows, page), 0)
        cc = lax.broadcasted_iota(jnp.int32, (rows, page), 1)
        s = jnp.where(cc <= rr % tq, s, -jnp.inf)
        m = jnp.max(s, axis=-1, keepdims=True)
        p = jnp.exp2(s - m)
        m_ref[...] = m
        l_ref[...] = jnp.sum(p, axis=-1, keepdims=True)
        acc_ref[...] = jnp.dot(p.astype(BF16), vn, preferred_element_type=F32)

    for c0 in range(0, len(k_refs), PAGE_CHUNK):
        for g in range(PAGE_CHUNK):
            kbuf[:, g * page:(g + 1) * page] = k_refs[c0 + g][...].astype(BF16)
            vh = pltpu.einshape("(th)d->htd", v_refs[c0 + g][...], h=n_heads)
            for h in range(n_heads):
                vbuf[g * page:(g + 1) * page, h * v_dim:(h + 1) * v_dim] = vh[h].astype(BF16)
        s = jnp.dot(qbd_ref[...], kbuf[...], preferred_element_type=F32)
        m = m_ref[...]
        m_new = jnp.maximum(m, jnp.max(s, axis=-1, keepdims=True))
        alpha = jnp.exp2(m - m_new)
        p = jnp.exp2(s - m_new)
        l_ref[...] = alpha * l_ref[...] + jnp.sum(p, axis=-1, keepdims=True)
        acc_ref[...] = alpha * acc_ref[...] + jnp.dot(p.astype(BF16), vbuf[...], preferred_element_type=F32)
        m_ref[...] = m_new

    @pl.when(s_idx == n_steps - 1)
    def _():
        o = acc_ref[...] / l_ref[...]
        for h in range(n_heads):
            cols = slice(h * v_dim, (h + 1) * v_dim)
            o1 = o[(2 * h) * tq:(2 * h + 1) * tq, cols]
            o2 = o[(2 * h + 1) * tq:(2 * h + 2) * tq, cols]
            o_ref[:, cols] = _head_out(o1, o2, lam, gsub_ref[...], lam_init)


def _attn_kernel(pt_ref, q_ref, kt_ref, v_ref, qs_ref, kn_ref, vn_ref, *rest,
                 n_pg, steps_per_sample, tq, n_heads, qk_dim, page, lam_init):
    del pt_ref
    k_refs = rest[:n_pg]
    v_refs = rest[n_pg:2 * n_pg]
    (lq1_ref, lk1_ref, lq2_ref, lk2_ref, gsub_ref, o_ref, os_ref,
     qbd_scr, s_scr, p_scr, m_scr, l_scr, a_scr, acc_scr,
     sqbd_scr, kbuf, vbuf, sm_scr, sl_scr, sacc_scr) = rest[2 * n_pg:]
    i = pl.program_id(2)
    step = (pl.program_id(0) * pl.num_programs(1) + pl.program_id(1)) * pl.num_programs(2) + i
    lam = _diff_lambda(lq1_ref, lk1_ref, lq2_ref, lk2_ref, lam_init)
    _sample_attn_body(step % steps_per_sample, steps_per_sample, qs_ref, kn_ref, vn_ref, k_refs, v_refs,
                      lam, gsub_ref, os_ref, sqbd_scr, kbuf, vbuf, sm_scr, sl_scr, sacc_scr,
                      n_heads=n_heads, qk_dim=qk_dim, page=page, lam_init=lam_init)
    _prompt_attn_body(i, q_ref, kt_ref, v_ref, lam, gsub_ref, o_ref,
                      qbd_scr, s_scr, p_scr, m_scr, l_scr, a_scr, acc_scr,
                      tq=tq, qk_dim=qk_dim, lam_init=lam_init, row_chunk=32)


def _attention(q, kt, v, qs, k_new, v_new, kt_pages, v_pages, page_table, lw, *,
               n_heads, qk_dim, lam_init, tq, n_hg):
    batch, seq, feat = q.shape
    dec_batch, dec_seq, _ = qs.shape
    page = kt_pages.shape[-1]
    n_pages = page_table.shape[1]
    v_dim = feat // n_heads
    n_groups = n_heads // n_hg
    n_q = seq // tq
    total_steps = batch * n_groups * n_q
    assert 2 * qk_dim == LANES and v_dim == LANES and page == LANES and n_heads % n_hg == 0
    assert (dec_batch * n_pages) % total_steps == 0
    n_pg = dec_batch * n_pages // total_steps
    assert n_pages % n_pg == 0 and n_pg % PAGE_CHUNK == 0
    steps_per_sample = n_pages // n_pg
    rows = 2 * tq
    srows = 2 * n_heads * dec_seq
    gw = n_hg * LANES

    def linear(b, h, i):
        return (b * n_groups + h) * n_q + i

    small = lambda n: pl.BlockSpec((1, n), lambda b, h, i, pt: (0, 0))
    new_spec = pl.BlockSpec((None, dec_seq, feat), lambda b, h, i, pt: (linear(b, h, i) // steps_per_sample, 0, 0))

    def page_spec(g, shape):
        def index(b, h, i, pt):
            n = linear(b, h, i)
            return (pt[n // steps_per_sample, (n % steps_per_sample) * n_pg + g], 0, 0)
        return pl.BlockSpec((None,) + shape, index)

    kern = functools.partial(_attn_kernel, n_pg=n_pg, steps_per_sample=steps_per_sample, tq=tq,
                             n_heads=n_heads, qk_dim=qk_dim, page=page, lam_init=lam_init)
    grid_spec = pltpu.PrefetchScalarGridSpec(
        num_scalar_prefetch=1,
        grid=(batch, n_groups, n_q),
        in_specs=[
            pl.BlockSpec((None, tq, gw), lambda b, h, i, pt: (b, i, h)),
            pl.BlockSpec((None, gw, seq), lambda b, h, i, pt: (b, h, 0), pipeline_mode=pl.Buffered(1)),
            pl.BlockSpec((None, seq, gw), lambda b, h, i, pt: (b, 0, h), pipeline_mode=pl.Buffered(1)),
            new_spec, new_spec, new_spec]
        + [page_spec(g, (feat, page)) for g in range(n_pg)]
        + [page_spec(g, (page * n_heads, v_dim)) for g in range(n_pg)]
        + [small(qk_dim)] * 4 + [small(v_dim)],
        out_specs=[pl.BlockSpec((None, tq, gw), lambda b, h, i, pt: (b, i, h)), new_spec],
        scratch_shapes=[
            pltpu.VMEM((n_hg, rows, 2 * qk_dim), BF16),
            pltpu.VMEM((n_hg, rows, tq), F32),
            pltpu.VMEM((n_hg, rows, tq), BF16),
            pltpu.VMEM((n_hg, rows, LANES), F32),
            pltpu.VMEM((n_hg, rows, LANES), F32),
            pltpu.VMEM((n_hg, rows, LANES), F32),
            pltpu.VMEM((n_hg, rows, v_dim), F32),
            pltpu.VMEM((srows, feat), BF16),
            pltpu.VMEM((feat, PAGE_CHUNK * page), BF16),
            pltpu.VMEM((PAGE_CHUNK * page, feat), BF16),
            pltpu.VMEM((srows, 1), F32),
            pltpu.VMEM((srows, 1), F32),
            pltpu.VMEM((srows, feat), F32),
        ],
    )
    return pl.pallas_call(
        kern,
        grid_spec=grid_spec,
        out_shape=[jax.ShapeDtypeStruct((batch, seq, feat), BF16),
                   jax.ShapeDtypeStruct((dec_batch, dec_seq, feat), F32)],
        compiler_params=pltpu.CompilerParams(
            dimension_semantics=("arbitrary", "arbitrary", "arbitrary"), vmem_limit_bytes=VMEM_LIMIT),
        name="attention",
    )(page_table, q, kt, v, qs, k_new, v_new, *([kt_pages] * n_pg), *([v_pages] * n_pg),
      lw['lq1'], lw['lk1'], lw['lq2'], lw['lk2'], lw['g_subln'])


def _mix_out_kernel(x_ref, oa_ref, rin_ref, sga_ref, sgr_ref, p_ref,
                    wattn_ref, wrec_ref, wout_ref, gffn_ref, wg_ref, wu_ref, wd_ref,
                    gple_ref, wpg_ref, wpp_ref, gfin_ref, y_ref, *, ff_chunk, final_norm):
    dot = functools.partial(jnp.dot, preferred_element_type=F32)
    o_a = dot(oa_ref[...].astype(BF16), wattn_ref[...])
    o_r = dot(rin_ref[...], wrec_ref[...])
    m = sga_ref[...] * o_a + sgr_ref[...] * o_r
    h = x_ref[...] + dot(m.astype(BF16), wout_ref[...])

    u2 = _rms(h, gffn_ref[...]).astype(BF16)
    d_ff = wg_ref.shape[1]
    ffn = jnp.zeros_like(h)
    for c0 in range(0, d_ff, ff_chunk):
        c1 = min(c0 + ff_chunk, d_ff)
        hid = jax.nn.silu(dot(u2, wg_ref[:, c0:c1])) * dot(u2, wu_ref[:, c0:c1])
        ffn = ffn + dot(hid.astype(BF16), wd_ref[c0:c1, :])
    h = h + ffn

    g = jax.nn.sigmoid(dot(_rms(h, gple_ref[...]).astype(BF16), wpg_ref[...]))
    h = h + g * dot(p_ref[...].astype(BF16), wpp_ref[...])
    y_ref[...] = _rms(h, gfin_ref[...]) if final_norm else h


def _mix_out(x2d, oa, rin, sga, sgr, p2d, lw, g_final, *, tm, final_norm):
    n_tok, d_model = x2d.shape
    tok_spec = lambda w: pl.BlockSpec((tm, w), lambda i: (i, 0))
    weights = [lw['w_attn_br'], lw['w_rec_br'], lw['w_out'], lw['g_ffn'], lw['w_ffn_gate'], lw['w_ffn_up'],
               lw['w_ffn_down'], lw['g_ple'], lw['w_ple_gate'], lw['w_ple_proj'], g_final]
    kern = functools.partial(_mix_out_kernel, ff_chunk=1024, final_norm=final_norm)
    return pl.pallas_call(
        kern,
        grid=(n_tok // tm,),
        in_specs=[tok_spec(d_model), tok_spec(oa.shape[1]), tok_spec(rin.shape[1]), tok_spec(d_model),
                  tok_spec(d_model), tok_spec(p2d.shape[1])] + [_resident(w.shape) for w in weights],
        out_specs=tok_spec(d_model),
        out_shape=jax.ShapeDtypeStruct((n_tok, d_model), F32),
        compiler_params=pltpu.CompilerParams(
            dimension_semantics=("arbitrary",), vmem_limit_bytes=VMEM_LIMIT),
        name="mix_out",
    )(x2d, oa, rin, sga, sgr, p2d, *weights)


def kernel(x_prompt, x_sample, p_prompt, p_sample, cache_k, cache_v, page_table, state_conv, state_h, g_mix, w_in, lambda_q1, lambda_k1, lambda_q2, lambda_k2, g_subln, w_attn_br, w_conv, b_conv, w_gate_a, b_gate_a, w_gate_x, b_gate_x, lru_lambda, w_rec_br, w_out, g_ffn, w_ffn_gate, w_ffn_up, w_ffn_down, g_ple, w_ple_gate, w_ple_proj, g_final):
    depth = w_in.shape[0]
    batch, seq, d_model = x_prompt.shape
    dec_batch, dec_seq, _ = x_sample.shape
    _, n_phys, page, n_heads, _, qk_dim = cache_k.shape
    v_dim = cache_v.shape[-1]
    width = w_conv.shape[-1]
    feat = n_heads * v_dim
    qk_scale = qk_dim ** -0.5 * math.log2(math.e)

    hp = x_prompt.reshape(batch * seq, d_model)
    hs = x_sample.reshape(dec_batch * dec_seq, d_model)
    outs = [[] for _ in range(8)]
    for l in range(depth):
        lam_init = 0.8 - 0.6 * math.exp(-0.3 * l)
        last = l == depth - 1
        lw = dict(
            g_mix=g_mix[l][None], w_in=w_in[l].astype(BF16),
            w_kt=w_in[l][:, feat:2 * feat].T.astype(BF16),
            lq1=lambda_q1[l][None], lk1=lambda_k1[l][None], lq2=lambda_q2[l][None], lk2=lambda_k2[l][None],
            g_subln=g_subln[l][None], w_attn_br=w_attn_br[l].astype(BF16),
            w_conv=w_conv[l], b_conv=b_conv[l][None],
            w_gate=jnp.concatenate([w_gate_a[l], w_gate_x[l]], axis=-1).astype(BF16),
            b_gate=jnp.concatenate([b_gate_a[l], b_gate_x[l]], axis=-1)[:, None, :],
            lru_lambda=lru_lambda[l][None], w_rec_br=w_rec_br[l].astype(BF16), w_out=w_out[l].astype(BF16),
            g_ffn=g_ffn[l][None], w_ffn_gate=w_ffn_gate[l].astype(BF16), w_ffn_up=w_ffn_up[l].astype(BF16),
            w_ffn_down=w_ffn_down[l].astype(BF16), g_ple=g_ple[l][None],
            w_ple_gate=w_ple_gate[l].astype(BF16), w_ple_proj=w_ple_proj[l].astype(BF16))
        gfin = g_final[None]

        conv0 = jnp.zeros((batch, CONV_WIDTH - 1, width), F32)
        h0 = jnp.zeros((batch, 1, width), F32)
        q, kf, vf, kb, vb, rin, sga, sgr, cnew, hnew = _mix_in(
            hp, conv0, h0, lw, batch=batch, seq=seq, nb=1, tt=256, qk_scale=qk_scale, k_transposed=True)
        qs, kfs, vfs, _, _, rins, sgas, sgrs, cnews, hnews = _mix_in(
            hs, state_conv[l], state_h[l][:, None, :], lw,
            batch=dec_batch, seq=dec_seq, nb=dec_batch, tt=dec_seq, qk_scale=qk_scale, k_transposed=False)

        kt_pages = jnp.transpose(cache_k[l], (0, 2, 3, 4, 1)).reshape(n_phys, feat, page)
        v_pages = cache_v[l].reshape(n_phys, page * n_heads, v_dim)
        oa, oas = _attention(
            q.reshape(batch, seq, feat), kb, vb.reshape(batch, seq, feat),
            qs.astype(F32).reshape(dec_batch, dec_seq, feat),
            kfs.reshape(dec_batch, dec_seq, feat), vfs.reshape(dec_batch, dec_seq, feat),
            kt_pages, v_pages, page_table, lw,
            n_heads=n_heads, qk_dim=qk_dim, lam_init=lam_init, tq=256, n_hg=4)

        hp = _mix_out(hp, oa.reshape(batch * seq, feat), rin, sga, sgr,
                      p_prompt[l].reshape(batch * seq, -1), lw, gfin, tm=256, final_norm=last)
        hs = _mix_out(hs, oas.reshape(dec_batch * dec_seq, feat), rins, sgas, sgrs,
                      p_sample[l].reshape(dec_batch * dec_seq, -1), lw, gfin,
                      tm=dec_batch * dec_seq, final_norm=last)

        outs[0].append(jnp.transpose(kf.reshape(batch, n_heads, 2, qk_dim, seq), (0, 4, 1, 2, 3)))
        outs[1].append(vf.reshape(batch, seq, n_heads, v_dim))
        outs[2].append(cnew)
        outs[3].append(hnew.reshape(batch, width))
        outs[4].append(kfs.reshape(dec_batch, dec_seq, n_heads, 2, qk_dim))
        outs[5].append(vfs.reshape(dec_batch, dec_seq, n_heads, v_dim))
        outs[6].append(cnews)
        outs[7].append(hnews.reshape(dec_batch, width))

    y_prompt = hp.reshape(batch, seq, d_model)
    y_sample = hs.reshape(dec_batch, dec_seq, d_model)
    return (y_prompt, y_sample) + tuple(jnp.stack(o) for o in outs)
```

```python
import functools
import math

import jax
import jax.numpy as jnp
from jax import lax
from jax.experimental import pallas as pl
from jax.experimental.pallas import tpu as pltpu

F32 = jnp.float32
BF16 = jnp.bfloat16

EPS = 1e-6
LRU_C = 8.0
CONV_WIDTH = 4
N_LRU_BLOCKS = 8
SUBLANES = 8
LANES = 128
CONV_PAD = 8
PAGE_CHUNK = 8
FILL_COLS = 256
VMEM_LIMIT = 56 * 1024 * 1024


def _rms(x, g):
    ms = jnp.mean(x * x, axis=-1, keepdims=True)
    return x * lax.rsqrt(ms + EPS) * g


def _softplus(x):
    return jnp.maximum(x, 0.0) + jnp.log1p(jnp.exp(-jnp.abs(x)))


def _resident(shape):
    nd = len(shape)
    return pl.BlockSpec(shape, lambda *_: (0,) * nd, pipeline_mode=pl.Buffered(1))


def _mix_in_kernel(x_ref, cprev_ref, hprev_ref, gmix_ref, win_ref, wkt_ref, wconv_ref, bconv_ref,
                   wgate_ref, bgate_ref, lam_ref,
                   q_ref, kf_ref, vf_ref, kb_ref, vb_ref, hs_ref, yr_ref, ga_ref, gr_ref,
                   cnew_ref, hnew_ref,
                   xbuf, abuf, ubuf, hstate, *, nb, tt, width, qk_scale, k_transposed):
    t = pl.program_id(1)
    tm = nb * tt
    lru_block = width // N_LRU_BLOCKS

    @pl.when(t == 0)
    def _():
        xbuf[:, CONV_PAD - (CONV_WIDTH - 1):CONV_PAD, :] = cprev_ref[...]
        hstate[...] = hprev_ref[...]

    u = _rms(x_ref[...], gmix_ref[...]).astype(BF16)

    def proj(i, c0=0, c1=width):
        return jnp.dot(u, win_ref[:, i * width + c0:i * width + c1], preferred_element_type=F32)

    def emit_q(c0, c1):
        q_ref[:, c0:c1] = (proj(0, c0, c1) * qk_scale).astype(BF16)

    def emit_k(c0, c1):
        if k_transposed:
            k = lax.dot_general(wkt_ref[c0:c1, :], u, (((1,), (1,)), ((), ())), preferred_element_type=F32)
            kf_ref[c0:c1, :] = k
            kb_ref[c0:c1, :] = k.astype(BF16)
        else:
            k = proj(1, c0, c1)
            kf_ref[:, c0:c1] = k
            kb_ref[:, c0:c1] = k.astype(BF16)

    def emit_v(c0, c1):
        v = proj(2, c0, c1)
        vf_ref[:, c0:c1] = v
        vb_ref[:, c0:c1] = v.astype(BF16)

    def emit_raw(i, ref, c0, c1):
        ref[:, c0:c1] = proj(i, c0, c1)

    fillers = []
    for c0 in range(0, width, FILL_COLS):
        c1 = c0 + FILL_COLS
        fillers += [functools.partial(emit_q, c0, c1), functools.partial(emit_k, c0, c1),
                    functools.partial(emit_v, c0, c1), functools.partial(emit_raw, 4, yr_ref, c0, c1),
                    functools.partial(emit_raw, 5, ga_ref, c0, c1), functools.partial(emit_raw, 6, gr_ref, c0, c1)]
    w_conv, w_gate, w_scan = 4, 6, 1
    total_weight = w_conv + N_LRU_BLOCKS * w_gate + nb * (tt // SUBLANES) * w_scan
    done = [0, 0]

    def fill(weight):
        done[0] += weight
        target = -(-len(fillers) * done[0] // total_weight)
        while done[1] < target:
            fillers[done[1]]()
            done[1] += 1

    xbuf[:, CONV_PAD:CONV_PAD + tt, :] = proj(3).reshape(nb, tt, width)
    fill(w_conv)
    xc = bconv_ref[...].reshape(1, 1, width)
    for j in range(CONV_WIDTH):
        r0 = CONV_PAD - (CONV_WIDTH - 1) + j
        xc = xc + xbuf[:, r0:r0 + tt, :] * wconv_ref[j:j + 1, :].reshape(1, 1, width)
    tail = xbuf[:, CONV_PAD + tt - (CONV_WIDTH - 1):CONV_PAD + tt, :]
    cnew_ref[...] = tail
    xbuf[:, CONV_PAD - (CONV_WIDTH - 1):CONV_PAD, :] = tail

    xc = xc.reshape(tm, width)
    xcb = xc.astype(BF16)
    sp = _softplus(-lam_ref[...])
    for blk in range(N_LRU_BLOCKS):
        fill(w_gate)
        sl = slice(blk * lru_block, (blk + 1) * lru_block)
        g = jnp.dot(xcb[:, sl], wgate_ref[blk], preferred_element_type=F32) + bgate_ref[blk]
        r = jax.nn.sigmoid(g[:, :lru_block])
        ig = jax.nn.sigmoid(g[:, lru_block:])
        a = jnp.exp(-LRU_C * r * sp[:, sl])
        abuf[:, sl] = a
        ubuf[:, sl] = jnp.sqrt(1.0 - a * a) * (ig * xc[:, sl])

    n_groups = tt // SUBLANES
    row = lax.broadcasted_iota(jnp.int32, (SUBLANES, width), 0)

    for b in range(nb):
        h = hstate[b]
        for g in range(n_groups):
            fill(w_scan)
            r0 = (b * n_groups + g) * SUBLANES
            a = abuf[r0:r0 + SUBLANES, :]
            uu = ubuf[r0:r0 + SUBLANES, :]
            d = 1
            while d < SUBLANES:
                a_sh = jnp.where(row >= d, pltpu.roll(a, d, 0), 1.0)
                u_sh = jnp.where(row >= d, pltpu.roll(uu, d, 0), 0.0)
                uu = a * u_sh + uu
                a = a * a_sh
                d *= 2
            hb = uu + a * h
            hs_ref[r0:r0 + SUBLANES, :] = hb
            h = hb[SUBLANES - 1:SUBLANES, :]
        hstate[b] = h
    hnew_ref[...] = hstate[...]
    assert done[1] == len(fillers)


def _mix_in(x2d, conv_prev, h_prev, lw, *, batch, seq, nb, tt, qk_scale, k_transposed):
    n_tok, d_model = x2d.shape
    width = lw['w_conv'].shape[-1]
    n_t = seq // tt
    tm = nb * tt
    grid = (batch // nb, n_t)
    tok = lambda dt: jax.ShapeDtypeStruct((n_tok, width), dt)
    tok_spec = pl.BlockSpec((tm, width), lambda b, t: (b * n_t + t, 0))
    if k_transposed:
        assert nb == 1
        k_shape = lambda dt: jax.ShapeDtypeStruct((batch, width, seq), dt)
        k_spec = pl.BlockSpec((None, width, tt), lambda b, t: (b, 0, t))
    else:
        k_shape, k_spec = tok, tok_spec
    state_c = pl.BlockSpec((nb, CONV_WIDTH - 1, width), lambda b, t: (b, 0, 0))
    state_h = pl.BlockSpec((nb, 1, width), lambda b, t: (b, 0, 0))
    kern = functools.partial(_mix_in_kernel, nb=nb, tt=tt, width=width, qk_scale=qk_scale,
                             k_transposed=k_transposed)
    return pl.pallas_call(
        kern,
        grid=grid,
        in_specs=[
            pl.BlockSpec((tm, d_model), lambda b, t: (b * n_t + t, 0)),
            state_c, state_h,
            _resident((1, d_model)),
            _resident(lw['w_in'].shape),
            _resident(lw['w_kt'].shape),
            _resident((CONV_WIDTH, width)),
            _resident((1, width)),
            _resident(lw['w_gate'].shape),
            _resident(lw['b_gate'].shape),
            _resident((1, width)),
        ],
        out_specs=[tok_spec, k_spec, tok_spec, k_spec] + [tok_spec] * 5 + [state_c, state_h],
        out_shape=[tok(BF16), k_shape(F32), tok(F32), k_shape(BF16), tok(BF16),
                   tok(F32), tok(F32), tok(F32), tok(F32),
                   jax.ShapeDtypeStruct((batch, CONV_WIDTH - 1, width), F32),
                   jax.ShapeDtypeStruct((batch, 1, width), F32)],
        scratch_shapes=[
            pltpu.VMEM((nb, CONV_PAD + tt, width), F32),
            pltpu.VMEM((tm, width), F32),
            pltpu.VMEM((tm, width), F32),
            pltpu.VMEM((nb, 1, width), F32),
        ],
        compiler_params=pltpu.CompilerParams(
            dimension_semantics=("arbitrary", "arbitrary"), vmem_limit_bytes=VMEM_LIMIT),
        name="mix_in",
    )(x2d, conv_prev, h_prev, lw['g_mix'], lw['w_in'], lw['w_kt'], lw['w_conv'], lw['b_conv'],
      lw['w_gate'], lw['b_gate'], lw['lru_lambda'])


def _diff_lambda(lq1_ref, lk1_ref, lq2_ref, lk2_ref, lam_init):
    s1 = jnp.sum(lq1_ref[...] * lk1_ref[...], axis=-1, keepdims=True)
    s2 = jnp.sum(lq2_ref[...] * lk2_ref[...], axis=-1, keepdims=True)
    return jnp.exp(s1) - jnp.exp(s2) + lam_init


def _head_out(o1, o2, lam, gsub, lam_init):
    od = o1 - lam * o2
    return _rms(od, gsub) * (1.0 - lam_init)


def _prompt_attn_body(i, q_ref, kt_ref, v_ref, lam, gsub_ref, o_ref,
                      qbd_scr, s_scr, p_scr, m_scr, l_scr, a_scr, acc_scr, *, tq, qk_dim, lam_init, row_chunk):
    tk = tq
    rows = 2 * tq
    n_hg = qbd_scr.shape[0]
    lane = lax.broadcasted_iota(jnp.int32, (tq, LANES), 1)
    for hh in range(n_hg):
        q = q_ref[:, hh * LANES:(hh + 1) * LANES]
        zero = jnp.zeros_like(q)
        qbd_scr[hh, 0:tq, :] = jnp.where(lane < qk_dim, q, zero)
        qbd_scr[hh, tq:rows, :] = jnp.where(lane >= qk_dim, q, zero)
    m_scr[...] = jnp.full(m_scr.shape, -jnp.inf, F32)
    l_scr[...] = jnp.zeros(l_scr.shape, F32)
    acc_scr[...] = jnp.zeros(acc_scr.shape, F32)

    def head_step(hh, k0, masked):
        hs = slice(hh * LANES, (hh + 1) * LANES)
        s_scr[hh] = jnp.dot(qbd_scr[hh], kt_ref[hs, pl.ds(k0, tk)], preferred_element_type=F32)
        for c in range(rows // row_chunk):
            r0 = c * row_chunk
            rs = slice(r0, r0 + row_chunk)
            s = s_scr[hh, rs, :]
            if masked:
                qpos = lax.broadcasted_iota(jnp.int32, (row_chunk, tk), 0) + (r0 % tq)
                kpos = lax.broadcasted_iota(jnp.int32, (row_chunk, tk), 1)
                s = jnp.where(kpos <= qpos, s, -jnp.inf)
            tiles = [s[:, t * LANES:(t + 1) * LANES] for t in range(tk // LANES)]
            mx = functools.reduce(jnp.maximum, tiles)
            m_old = m_scr[hh, rs, :]
            m_new = jnp.maximum(m_old, jnp.max(mx, axis=-1, keepdims=True))
            alpha = jnp.exp2(m_old - m_new)
            ps = [jnp.exp2(t - m_new) for t in tiles]
            l_scr[hh, rs, :] = alpha * l_scr[hh, rs, :] + functools.reduce(jnp.add, ps)
            m_scr[hh, rs, :] = m_new
            a_scr[hh, rs, :] = alpha
            p_scr[hh, rs, :] = jnp.concatenate(ps, axis=1).astype(BF16)
        pv = jnp.dot(p_scr[hh], v_ref[pl.ds(k0, tk), hs], preferred_element_type=F32)
        acc_scr[hh] = a_scr[hh] * acc_scr[hh] + pv

    def step(j, masked):
        k0 = pl.multiple_of(j * tk, tk)
        for hh in range(n_hg):
            head_step(hh, k0, masked)

    def unmasked(j, carry):
        step(j, False)
        return carry

    lax.fori_loop(0, i, unmasked, 0)
    step(i, True)
    for hh in range(n_hg):
        o = acc_scr[hh] / jnp.sum(l_scr[hh], axis=-1, keepdims=True)
        o_ref[:, hh * LANES:(hh + 1) * LANES] = _head_out(
            o[:tq], o[tq:], lam, gsub_ref[...], lam_init).astype(o_ref.dtype)


def _sample_attn_body(s_idx, n_steps, q_ref, kn_ref, vn_ref, k_refs, v_refs, lam, gsub_ref, o_ref,
                      qbd_ref, kbuf, vbuf, m_ref, l_ref, acc_ref, *, n_heads, qk_dim, page, lam_init):
    tq, feat = q_ref.shape
    n_maps = 2 * n_heads
    rows = n_maps * tq
    v_dim = feat // n_heads

    @pl.when(s_idx == 0)
    def _():
        qt = jnp.concatenate([q_ref[...]] * n_maps, axis=0)
        r = lax.broadcasted_iota(jnp.int32, (rows, feat), 0)
        c = lax.broadcasted_iota(jnp.int32, (rows, feat), 1)
        qbd = jnp.where(r // tq == c // qk_dim, qt, 0.0).astype(BF16)
        qbd_ref[...] = qbd
        pad = jnp.zeros((page - tq, feat), F32)
        kn = jnp.concatenate([kn_ref[...], pad], axis=0).astype(BF16)
        vn = jnp.concatenate([vn_ref[...], pad], axis=0).astype(BF16)
        s = lax.dot_general(qbd, kn, (((1,), (1,)), ((), ())), preferred_element_type=F32)
        rr = lax.broadcasted_iota(jnp.int32, (rows, page), 0)
        cc = lax.broadcasted_iota(jnp.int32, (rows, page), 1)
        s = jnp.where(cc <= rr % tq, s, -jnp.inf)
        m = jnp.max(s, axis=-1, keepdims=True)
        p = jnp.exp2(s - m)
        m_ref[...] = m
        l_ref[...] = jnp.sum(p, axis=-1, keepdims=True)
        acc_ref[...] = jnp.dot(p.astype(BF16), vn, preferred_element_type=F32)

    for c0 in range(0, len(k_refs), PAGE_CHUNK):
        for g in range(PAGE_CHUNK):
            kbuf[:, g * page:(g + 1) * page] = k_refs[c0 + g][...].astype(BF16)
            for h in range(n_heads):
                vh = v_refs[c0 + g][pl.ds(h, page, stride=n_heads), :]
                vbuf[g * page:(g + 1) * page, h * v_dim:(h + 1) * v_dim] = vh.astype(BF16)
        s = jnp.dot(qbd_ref[...], kbuf[...], preferred_element_type=F32)
        m = m_ref[...]
        m_new = jnp.maximum(m, jnp.max(s, axis=-1, keepdims=True))
        alpha = jnp.exp2(m - m_new)
        p = jnp.exp2(s - m_new)
        l_ref[...] = alpha * l_ref[...] + jnp.sum(p, axis=-1, keepdims=True)
        acc_ref[...] = alpha * acc_ref[...] + jnp.dot(p.astype(BF16), vbuf[...], preferred_element_type=F32)
        m_ref[...] = m_new

    @pl.when(s_idx == n_steps - 1)
    def _():
        o = acc_ref[...] / l_ref[...]
        for h in range(n_heads):
            cols = slice(h * v_dim, (h + 1) * v_dim)
            o1 = o[(2 * h) * tq:(2 * h + 1) * tq, cols]
            o2 = o[(2 * h + 1) * tq:(2 * h + 2) * tq, cols]
            o_ref[:, cols] = _head_out(o1, o2, lam, gsub_ref[...], lam_init)


def _attn_kernel(pt_ref, q_ref, kt_ref, v_ref, qs_ref, kn_ref, vn_ref, *rest,
                 n_pg, steps_per_sample, tq, n_heads, qk_dim, page, lam_init):
    del pt_ref
    k_refs = rest[:n_pg]
    v_refs = rest[n_pg:2 * n_pg]
    (lq1_ref, lk1_ref, lq2_ref, lk2_ref, gsub_ref, o_ref, os_ref,
     qbd_scr, s_scr, p_scr, m_scr, l_scr, a_scr, acc_scr,
     sqbd_scr, kbuf, vbuf, sm_scr, sl_scr, sacc_scr) = rest[2 * n_pg:]
    i = pl.program_id(2)
    step = (pl.program_id(0) * pl.num_programs(1) + pl.program_id(1)) * pl.num_programs(2) + i
    lam = _diff_lambda(lq1_ref, lk1_ref, lq2_ref, lk2_ref, lam_init)
    _sample_attn_body(step % steps_per_sample, steps_per_sample, qs_ref, kn_ref, vn_ref, k_refs, v_refs,
                      lam, gsub_ref, os_ref, sqbd_scr, kbuf, vbuf, sm_scr, sl_scr, sacc_scr,
                      n_heads=n_heads, qk_dim=qk_dim, page=page, lam_init=lam_init)
    _prompt_attn_body(i, q_ref, kt_ref, v_ref, lam, gsub_ref, o_ref,
                      qbd_scr, s_scr, p_scr, m_scr, l_scr, a_scr, acc_scr,
                      tq=tq, qk_dim=qk_dim, lam_init=lam_init, row_chunk=32)


def _attention(q, kt, v, qs, k_new, v_new, kt_pages, v_pages, page_table, lw, *,
               n_heads, qk_dim, lam_init, tq, n_hg):
    batch, seq, feat = q.shape
    dec_batch, dec_seq, _ = qs.shape
    page = kt_pages.shape[-1]
    n_pages = page_table.shape[1]
    v_dim = feat // n_heads
    n_groups = n_heads // n_hg
    n_q = seq // tq
    total_steps = batch * n_groups * n_q
    assert 2 * qk_dim == LANES and v_dim == LANES and page == LANES and n_heads % n_hg == 0
    assert (dec_batch * n_pages) % total_steps == 0
    n_pg = dec_batch * n_pages // total_steps
    assert n_pages % n_pg == 0 and n_pg % PAGE_CHUNK == 0
    steps_per_sample = n_pages // n_pg
    rows = 2 * tq
    srows = 2 * n_heads * dec_seq
    gw = n_hg * LANES

    def linear(b, h, i):
        return (b * n_groups + h) * n_q + i

    small = lambda n: pl.BlockSpec((1, n), lambda b, h, i, pt: (0, 0))
    new_spec = pl.BlockSpec((None, dec_seq, feat), lambda b, h, i, pt: (linear(b, h, i) // steps_per_sample, 0, 0))

    def page_spec(g, shape):
        def index(b, h, i, pt):
            n = linear(b, h, i)
            return (pt[n // steps_per_sample, (n % steps_per_sample) * n_pg + g], 0, 0)
        return pl.BlockSpec((None,) + shape, index)

    kern = functools.partial(_attn_kernel, n_pg=n_pg, steps_per_sample=steps_per_sample, tq=tq,
                             n_heads=n_heads, qk_dim=qk_dim, page=page, lam_init=lam_init)
    grid_spec = pltpu.PrefetchScalarGridSpec(
        num_scalar_prefetch=1,
        grid=(batch, n_groups, n_q),
        in_specs=[
            pl.BlockSpec((None, tq, gw), lambda b, h, i, pt: (b, i, h)),
            pl.BlockSpec((None, gw, seq), lambda b, h, i, pt: (b, h, 0), pipeline_mode=pl.Buffered(1)),
            pl.BlockSpec((None, seq, gw), lambda b, h, i, pt: (b, 0, h), pipeline_mode=pl.Buffered(1)),
            new_spec, new_spec, new_spec]
        + [page_spec(g, (feat, page)) for g in range(n_pg)]
        + [page_spec(g, (page * n_heads, v_dim)) for g in range(n_pg)]
        + [small(qk_dim)] * 4 + [small(v_dim)],
        out_specs=[pl.BlockSpec((None, tq, gw), lambda b, h, i, pt: (b, i, h)), new_spec],
        scratch_shapes=[
            pltpu.VMEM((n_hg, rows, 2 * qk_dim), BF16),
            pltpu.VMEM((n_hg, rows, tq), F32),
            pltpu.VMEM((n_hg, rows, tq), BF16),
            pltpu.VMEM((n_hg, rows, LANES), F32),
            pltpu.VMEM((n_hg, rows, LANES), F32),
            pltpu.VMEM((n_hg, rows, LANES), F32),
            pltpu.VMEM((n_hg, rows, v_dim), F32),
            pltpu.VMEM((srows, feat), BF16),
            pltpu.VMEM((feat, PAGE_CHUNK * page), BF16),
            pltpu.VMEM((PAGE_CHUNK * page, feat), BF16),
            pltpu.VMEM((srows, 1), F32),
            pltpu.VMEM((srows, 1), F32),
            pltpu.VMEM((srows, feat), F32),
        ],
    )
    return pl.pallas_call(
        kern,
        grid_spec=grid_spec,
        out_shape=[jax.ShapeDtypeStruct((batch, seq, feat), BF16),
                   jax.ShapeDtypeStruct((dec_batch, dec_seq, feat), F32)],
        compiler_params=pltpu.CompilerParams(
            dimension_semantics=("arbitrary", "arbitrary", "arbitrary"), vmem_limit_bytes=VMEM_LIMIT),
        name="attention",
    )(page_table, q, kt, v, qs, k_new, v_new, *([kt_pages] * n_pg), *([v_pages] * n_pg),
      lw['lq1'], lw['lk1'], lw['lq2'], lw['lk2'], lw['g_subln'])


def _mix_out_kernel(x_ref, oa_ref, hs_ref, yr_ref, ga_ref, gr_ref, p_ref,
                    wattn_ref, wrec_ref, wout_ref, gffn_ref, wg_ref, wu_ref, wd_ref,
                    gple_ref, wpg_ref, wpp_ref, gfin_ref, y_ref, *, ff_chunk, final_norm):
    dot = functools.partial(jnp.dot, preferred_element_type=F32)
    o_a = dot(oa_ref[...].astype(BF16), wattn_ref[...])
    o_r = dot((hs_ref[...] * jax.nn.gelu(yr_ref[...])).astype(BF16), wrec_ref[...])
    m = jax.nn.sigmoid(ga_ref[...]) * o_a + jax.nn.sigmoid(gr_ref[...]) * o_r
    h = x_ref[...] + dot(m.astype(BF16), wout_ref[...])

    u2 = _rms(h, gffn_ref[...]).astype(BF16)
    d_ff = wg_ref.shape[1]
    ffn = jnp.zeros_like(h)
    for c0 in range(0, d_ff, ff_chunk):
        c1 = min(c0 + ff_chunk, d_ff)
        hid = jax.nn.silu(dot(u2, wg_ref[:, c0:c1])) * dot(u2, wu_ref[:, c0:c1])
        ffn = ffn + dot(hid.astype(BF16), wd_ref[c0:c1, :])
    h = h + ffn

    g = jax.nn.sigmoid(dot(_rms(h, gple_ref[...]).astype(BF16), wpg_ref[...]))
    h = h + g * dot(p_ref[...].astype(BF16), wpp_ref[...])
    y_ref[...] = _rms(h, gfin_ref[...]) if final_norm else h


def _mix_out(x2d, oa, hs, yr, ga, gr, p2d, lw, g_final, *, tm, final_norm):
    n_tok, d_model = x2d.shape
    tok_spec = lambda w: pl.BlockSpec((tm, w), lambda i: (i, 0))
    weights = [lw['w_attn_br'], lw['w_rec_br'], lw['w_out'], lw['g_ffn'], lw['w_ffn_gate'], lw['w_ffn_up'],
               lw['w_ffn_down'], lw['g_ple'], lw['w_ple_gate'], lw['w_ple_proj'], g_final]
    kern = functools.partial(_mix_out_kernel, ff_chunk=1024, final_norm=final_norm)
    return pl.pallas_call(
        kern,
        grid=(n_tok // tm,),
        in_specs=[tok_spec(d_model), tok_spec(oa.shape[1]), tok_spec(hs.shape[1]), tok_spec(yr.shape[1]),
                  tok_spec(d_model), tok_spec(d_model), tok_spec(p2d.shape[1])]
        + [_resident(w.shape) for w in weights],
        out_specs=tok_spec(d_model),
        out_shape=jax.ShapeDtypeStruct((n_tok, d_model), F32),
        compiler_params=pltpu.CompilerParams(
            dimension_semantics=("arbitrary",), vmem_limit_bytes=VMEM_LIMIT),
        name="mix_out",
    )(x2d, oa, hs, yr, ga, gr, p2d, *weights)


def kernel(x_prompt, x_sample, p_prompt, p_sample, cache_k, cache_v, page_table, state_conv, state_h, g_mix, w_in, lambda_q1, lambda_k1, lambda_q2, lambda_k2, g_subln, w_attn_br, w_conv, b_conv, w_gate_a, b_gate_a, w_gate_x, b_gate_x, lru_lambda, w_rec_br, w_out, g_ffn, w_ffn_gate, w_ffn_up, w_ffn_down, g_ple, w_ple_gate, w_ple_proj, g_final):
    depth = w_in.shape[0]
    batch, seq, d_model = x_prompt.shape
    dec_batch, dec_seq, _ = x_sample.shape
    _, n_phys, page, n_heads, _, qk_dim = cache_k.shape
    v_dim = cache_v.shape[-1]
    width = w_conv.shape[-1]
    feat = n_heads * v_dim
    qk_scale = qk_dim ** -0.5 * math.log2(math.e)

    hp = x_prompt.reshape(batch * seq, d_model)
    hs = x_sample.reshape(dec_batch * dec_seq, d_model)
    outs = [[] for _ in range(8)]
    for l in range(depth):
        lam_init = 0.8 - 0.6 * math.exp(-0.3 * l)
        last = l == depth - 1
        lw = dict(
            g_mix=g_mix[l][None], w_in=w_in[l].astype(BF16),
            w_kt=w_in[l][:, feat:2 * feat].T.astype(BF16),
            lq1=lambda_q1[l][None], lk1=lambda_k1[l][None], lq2=lambda_q2[l][None], lk2=lambda_k2[l][None],
            g_subln=g_subln[l][None], w_attn_br=w_attn_br[l].astype(BF16),
            w_conv=w_conv[l], b_conv=b_conv[l][None],
            w_gate=jnp.concatenate([w_gate_a[l], w_gate_x[l]], axis=-1).astype(BF16),
            b_gate=jnp.concatenate([b_gate_a[l], b_gate_x[l]], axis=-1)[:, None, :],
            lru_lambda=lru_lambda[l][None], w_rec_br=w_rec_br[l].astype(BF16), w_out=w_out[l].astype(BF16),
            g_ffn=g_ffn[l][None], w_ffn_gate=w_ffn_gate[l].astype(BF16), w_ffn_up=w_ffn_up[l].astype(BF16),
            w_ffn_down=w_ffn_down[l].astype(BF16), g_ple=g_ple[l][None],
            w_ple_gate=w_ple_gate[l].astype(BF16), w_ple_proj=w_ple_proj[l].astype(BF16))
        gfin = g_final[None]

        conv0 = jnp.zeros((batch, CONV_WIDTH - 1, width), F32)
        h0 = jnp.zeros((batch, 1, width), F32)
        q, kf, vf, kb, vb, rec, yr, ga, gr, cnew, hnew = _mix_in(
            hp, conv0, h0, lw, batch=batch, seq=seq, nb=1, tt=256, qk_scale=qk_scale, k_transposed=True)
        qs, kfs, vfs, _, _, recs, yrs, gas, grs, cnews, hnews = _mix_in(
            hs, state_conv[l], state_h[l][:, None, :], lw,
            batch=dec_batch, seq=dec_seq, nb=dec_batch, tt=dec_seq, qk_scale=qk_scale, k_transposed=False)

        kt_pages = jnp.transpose(cache_k[l], (0, 2, 3, 4, 1)).reshape(n_phys, feat, page)
        v_pages = cache_v[l].reshape(n_phys, page * n_heads, v_dim)
        oa, oas = _attention(
            q.reshape(batch, seq, feat), kb, vb.reshape(batch, seq, feat),
            qs.astype(F32).reshape(dec_batch, dec_seq, feat),
            kfs.reshape(dec_batch, dec_seq, feat), vfs.reshape(dec_batch, dec_seq, feat),
            kt_pages, v_pages, page_table, lw,
            n_heads=n_heads, qk_dim=qk_dim, lam_init=lam_init, tq=256, n_hg=4)

        hp = _mix_out(hp, oa.reshape(batch * seq, feat), rec, yr, ga, gr,
                      p_prompt[l].reshape(batch * seq, -1), lw, gfin, tm=256, final_norm=last)
        hs = _mix_out(hs, oas.reshape(dec_batch * dec_seq, feat), recs, yrs, gas, grs,
                      p_sample[l].reshape(dec_batch * dec_seq, -1), lw, gfin,
                      tm=dec_batch * dec_seq, final_norm=last)

        outs[0].append(jnp.transpose(kf.reshape(batch, n_heads, 2, qk_dim, seq), (0, 4, 1, 2, 3)))
        outs[1].append(vf.reshape(batch, seq, n_heads, v_dim))
        outs[2].append(cnew)
        outs[3].append(hnew.reshape(batch, width))
        outs[4].append(kfs.reshape(dec_batch, dec_seq, n_heads, 2, qk_dim))
        outs[5].append(vfs.reshape(dec_batch, dec_seq, n_heads, v_dim))
        outs[6].append(cnews)
        outs[7].append(hnews.reshape(dec_batch, width))

    y_prompt = hp.reshape(batch, seq, d_model)
    y_sample = hs.reshape(dec_batch, dec_seq, d_model)
    return (y_prompt, y_sample) + tuple(jnp.stack(o) for o in outs)
```

```python
import functools
import math

import jax
import jax.numpy as jnp
from jax import lax
from jax.experimental import pallas as pl
from jax.experimental.pallas import tpu as pltpu

F32 = jnp.float32
BF16 = jnp.bfloat16

EPS = 1e-6
LRU_C = 8.0
CONV_WIDTH = 4
N_LRU_BLOCKS = 8
SUBLANES = 8
LANES = 128
CONV_PAD = 8
PAGE_CHUNK = 8
FILL_COLS = 256
VMEM_LIMIT = 56 * 1024 * 1024


def _rms(x, g):
    ms = jnp.mean(x * x, axis=-1, keepdims=True)
    return x * lax.rsqrt(ms + EPS) * g


def _softplus(x):
    return jnp.maximum(x, 0.0) + jnp.log1p(jnp.exp(-jnp.abs(x)))


def _resident(shape):
    nd = len(shape)
    return pl.BlockSpec(shape, lambda *_: (0,) * nd, pipeline_mode=pl.Buffered(1))


def _mix_in_kernel(x_ref, cprev_ref, hprev_ref, gmix_ref, win_ref, wkt_ref, wconv_ref, bconv_ref,
                   wgate_ref, bgate_ref, lam_ref,
                   q_ref, kf_ref, vf_ref, kb_ref, vb_ref, hs_ref, yr_ref, ga_ref, gr_ref,
                   cnew_ref, hnew_ref,
                   xbuf, abuf, ubuf, hstate, *, nb, tt, width, qk_scale, k_transposed):
    t = pl.program_id(1)
    tm = nb * tt
    lru_block = width // N_LRU_BLOCKS

    @pl.when(t == 0)
    def _():
        xbuf[:, CONV_PAD - (CONV_WIDTH - 1):CONV_PAD, :] = cprev_ref[...]
        hstate[...] = hprev_ref[...]

    u = _rms(x_ref[...], gmix_ref[...]).astype(BF16)

    def proj(i, c0=0, c1=width):
        return jnp.dot(u, win_ref[:, i * width + c0:i * width + c1], preferred_element_type=F32)

    def emit_q(c0, c1):
        q_ref[:, c0:c1] = (proj(0, c0, c1) * qk_scale).astype(BF16)

    def emit_k(c0, c1):
        if k_transposed:
            k = lax.dot_general(wkt_ref[c0:c1, :], u, (((1,), (1,)), ((), ())), preferred_element_type=F32)
            kf_ref[c0:c1, :] = k
            kb_ref[c0:c1, :] = k.astype(BF16)
        else:
            k = proj(1, c0, c1)
            kf_ref[:, c0:c1] = k
            kb_ref[:, c0:c1] = k.astype(BF16)

    def emit_v(c0, c1):
        v = proj(2, c0, c1)
        vf_ref[:, c0:c1] = v
        vb_ref[:, c0:c1] = v.astype(BF16)

    def emit_raw(i, ref, c0, c1):
        ref[:, c0:c1] = proj(i, c0, c1)

    fillers = []
    for c0 in range(0, width, FILL_COLS):
        c1 = c0 + FILL_COLS
        fillers += [functools.partial(emit_q, c0, c1), functools.partial(emit_k, c0, c1),
                    functools.partial(emit_v, c0, c1), functools.partial(emit_raw, 4, yr_ref, c0, c1),
                    functools.partial(emit_raw, 5, ga_ref, c0, c1), functools.partial(emit_raw, 6, gr_ref, c0, c1)]
    w_conv, w_gate, w_scan = 4, 6, 1
    total_weight = w_conv + N_LRU_BLOCKS * w_gate + nb * (tt // SUBLANES) * w_scan
    done = [0, 0]

    def fill(weight):
        done[0] += weight
        target = -(-len(fillers) * done[0] // total_weight)
        while done[1] < target:
            fillers[done[1]]()
            done[1] += 1

    xbuf[:, CONV_PAD:CONV_PAD + tt, :] = proj(3).reshape(nb, tt, width)
    fill(w_conv)
    xc = bconv_ref[...].reshape(1, 1, width)
    for j in range(CONV_WIDTH):
        r0 = CONV_PAD - (CONV_WIDTH - 1) + j
        xc = xc + xbuf[:, r0:r0 + tt, :] * wconv_ref[j:j + 1, :].reshape(1, 1, width)
    tail = xbuf[:, CONV_PAD + tt - (CONV_WIDTH - 1):CONV_PAD + tt, :]
    cnew_ref[...] = tail
    xbuf[:, CONV_PAD - (CONV_WIDTH - 1):CONV_PAD, :] = tail

    xc = xc.reshape(tm, width)
    xcb = xc.astype(BF16)
    sp = _softplus(-lam_ref[...])
    for blk in range(N_LRU_BLOCKS):
        fill(w_gate)
        sl = slice(blk * lru_block, (blk + 1) * lru_block)
        g = jnp.dot(xcb[:, sl], wgate_ref[blk], preferred_element_type=F32) + bgate_ref[blk]
        r = jax.nn.sigmoid(g[:, :lru_block])
        ig = jax.nn.sigmoid(g[:, lru_block:])
        a = jnp.exp(-LRU_C * r * sp[:, sl])
        abuf[:, sl] = a
        ubuf[:, sl] = jnp.sqrt(1.0 - a * a) * (ig * xc[:, sl])

    n_groups = tt // SUBLANES
    row = lax.broadcasted_iota(jnp.int32, (SUBLANES, width), 0)

    for b in range(nb):
        h = hstate[b]
        for g in range(n_groups):
            fill(w_scan)
            r0 = (b * n_groups + g) * SUBLANES
            a = abuf[r0:r0 + SUBLANES, :]
            uu = ubuf[r0:r0 + SUBLANES, :]
            d = 1
            while d < SUBLANES:
                a_sh = jnp.where(row >= d, pltpu.roll(a, d, 0), 1.0)
                u_sh = jnp.where(row >= d, pltpu.roll(uu, d, 0), 0.0)
                uu = a * u_sh + uu
                a = a * a_sh
                d *= 2
            hb = uu + a * h
            hs_ref[r0:r0 + SUBLANES, :] = hb
            h = hb[SUBLANES - 1:SUBLANES, :]
        hstate[b] = h
    hnew_ref[...] = hstate[...]
    assert done[1] == len(fillers)


def _mix_in(x2d, conv_prev, h_prev, lw, *, batch, seq, nb, tt, qk_scale, k_transposed):
    n_tok, d_model = x2d.shape
    width = lw['w_conv'].shape[-1]
    n_t = seq // tt
    tm = nb * tt
    grid = (batch // nb, n_t)
    tok = lambda dt: jax.ShapeDtypeStruct((n_tok, width), dt)
    tok_spec = pl.BlockSpec((tm, width), lambda b, t: (b * n_t + t, 0))
    if k_transposed:
        assert nb == 1
        k_shape = lambda dt: jax.ShapeDtypeStruct((batch, width, seq), dt)
        k_spec = pl.BlockSpec((None, width, tt), lambda b, t: (b, 0, t))
    else:
        k_shape, k_spec = tok, tok_spec
    state_c = pl.BlockSpec((nb, CONV_WIDTH - 1, width), lambda b, t: (b, 0, 0))
    state_h = pl.BlockSpec((nb, 1, width), lambda b, t: (b, 0, 0))
    kern = functools.partial(_mix_in_kernel, nb=nb, tt=tt, width=width, qk_scale=qk_scale,
                             k_transposed=k_transposed)
    return pl.pallas_call(
        kern,
        grid=grid,
        in_specs=[
            pl.BlockSpec((tm, d_model), lambda b, t: (b * n_t + t, 0)),
            state_c, state_h,
            _resident((1, d_model)),
            _resident(lw['w_in'].shape),
            _resident(lw['w_kt'].shape),
            _resident((CONV_WIDTH, width)),
            _resident((1, width)),
            _resident(lw['w_gate'].shape),
            _resident(lw['b_gate'].shape),
            _resident((1, width)),
        ],
        out_specs=[tok_spec, k_spec, tok_spec, k_spec] + [tok_spec] * 5 + [state_c, state_h],
        out_shape=[tok(BF16), k_shape(F32), tok(F32), k_shape(BF16), tok(BF16),
                   tok(F32), tok(F32), tok(F32), tok(F32),
                   jax.ShapeDtypeStruct((batch, CONV_WIDTH - 1, width), F32),
                   jax.ShapeDtypeStruct((batch, 1, width), F32)],
        scratch_shapes=[
            pltpu.VMEM((nb, CONV_PAD + tt, width), F32),
            pltpu.VMEM((tm, width), F32),
            pltpu.VMEM((tm, width), F32),
            pltpu.VMEM((nb, 1, width), F32),
        ],
        compiler_params=pltpu.CompilerParams(
            dimension_semantics=("arbitrary", "arbitrary"), vmem_limit_bytes=VMEM_LIMIT),
        name="mix_in",
    )(x2d, conv_prev, h_prev, lw['g_mix'], lw['w_in'], lw['w_kt'], lw['w_conv'], lw['b_conv'],
      lw['w_gate'], lw['b_gate'], lw['lru_lambda'])


def _diff_lambda(lq1_ref, lk1_ref, lq2_ref, lk2_ref, lam_init):
    s1 = jnp.sum(lq1_ref[...] * lk1_ref[...], axis=-1, keepdims=True)
    s2 = jnp.sum(lq2_ref[...] * lk2_ref[...], axis=-1, keepdims=True)
    return jnp.exp(s1) - jnp.exp(s2) + lam_init


def _head_out(o1, o2, lam, gsub, lam_init):
    od = o1 - lam * o2
    return _rms(od, gsub) * (1.0 - lam_init)


def _prompt_attn_body(i, q_ref, kt_ref, v_ref, lam, gsub_ref, o_ref,
                      qbd_scr, s_scr, p_scr, m_scr, l_scr, a_scr, acc_scr, *, tq, qk_dim, lam_init, row_chunk):
    tk = tq
    rows = 2 * tq
    n_hg = qbd_scr.shape[0]
    lane = lax.broadcasted_iota(jnp.int32, (tq, LANES), 1)
    for hh in range(n_hg):
        q = q_ref[:, hh * LANES:(hh + 1) * LANES]
        zero = jnp.zeros_like(q)
        qbd_scr[hh, 0:tq, :] = jnp.where(lane < qk_dim, q, zero)
        qbd_scr[hh, tq:rows, :] = jnp.where(lane >= qk_dim, q, zero)
    m_scr[...] = jnp.full(m_scr.shape, -jnp.inf, F32)
    l_scr[...] = jnp.zeros(l_scr.shape, F32)
    acc_scr[...] = jnp.zeros(acc_scr.shape, F32)

    def head_step(hh, k0, masked):
        hs = slice(hh * LANES, (hh + 1) * LANES)
        s_scr[hh] = jnp.dot(qbd_scr[hh], kt_ref[hs, pl.ds(k0, tk)], preferred_element_type=F32)
        for c in range(rows // row_chunk):
            r0 = c * row_chunk
            rs = slice(r0, r0 + row_chunk)
            s = s_scr[hh, rs, :]
            if masked:
                qpos = lax.broadcasted_iota(jnp.int32, (row_chunk, tk), 0) + (r0 % tq)
                kpos = lax.broadcasted_iota(jnp.int32, (row_chunk, tk), 1)
                s = jnp.where(kpos <= qpos, s, -jnp.inf)
            tiles = [s[:, t * LANES:(t + 1) * LANES] for t in range(tk // LANES)]
            mx = functools.reduce(jnp.maximum, tiles)
            m_old = m_scr[hh, rs, :]
            m_new = jnp.maximum(m_old, jnp.max(mx, axis=-1, keepdims=True))
            alpha = jnp.exp2(m_old - m_new)
            ps = [jnp.exp2(t - m_new) for t in tiles]
            l_scr[hh, rs, :] = alpha * l_scr[hh, rs, :] + functools.reduce(jnp.add, ps)
            m_scr[hh, rs, :] = m_new
            a_scr[hh, rs, :] = alpha
            p_scr[hh, rs, :] = jnp.concatenate(ps, axis=1).astype(BF16)
        pv = jnp.dot(p_scr[hh], v_ref[pl.ds(k0, tk), hs], preferred_element_type=F32)
        acc_scr[hh] = a_scr[hh] * acc_scr[hh] + pv

    def step(j, masked):
        k0 = pl.multiple_of(j * tk, tk)
        for hh in range(n_hg):
            head_step(hh, k0, masked)

    def unmasked(j, carry):
        step(j, False)
        return carry

    lax.fori_loop(0, i, unmasked, 0)
    step(i, True)
    for hh in range(n_hg):
        o = acc_scr[hh] / jnp.sum(l_scr[hh], axis=-1, keepdims=True)
        o_ref[:, hh * LANES:(hh + 1) * LANES] = _head_out(
            o[:tq], o[tq:], lam, gsub_ref[...], lam_init).astype(o_ref.dtype)


def _sample_attn_body(s_idx, n_steps, q_ref, kn_ref, vn_ref, k_refs, v_refs, lam, gsub_ref, o_ref,
                      qbd_ref, kbuf, vbuf, m_ref, l_ref, acc_ref, *, n_heads, qk_dim, page, lam_init):
    tq, feat = q_ref.shape
    n_maps = 2 * n_heads
    rows = n_maps * tq
    v_dim = feat // n_heads

    @pl.when(s_idx == 0)
    def _():
        qt = jnp.concatenate([q_ref[...]] * n_maps, axis=0)
        r = lax.broadcasted_iota(jnp.int32, (rows, feat), 0)
        c = lax.broadcasted_iota(jnp.int32, (rows, feat), 1)
        qbd = jnp.where(r // tq == c // qk_dim, qt, 0.0).astype(BF16)
        qbd_ref[...] = qbd
        pad = jnp.zeros((page - tq, feat), F32)
        kn = jnp.concatenate([kn_ref[...], pad], axis=0).astype(BF16)
        vn = jnp.concatenate([vn_ref[...], pad], axis=0).astype(BF16)
        s = lax.dot_general(qbd, kn, (((1,), (1,)), ((), ())), preferred_element_type=F32)
        rr = lax.broadcasted_iota(jnp.int32, (rows, page), 0)
        cc = lax.broadcasted_iota(jnp.int32, (rows, page), 1)
        s = jnp.where(cc <= rr % tq, s, -jnp.inf)
        m = jnp.max(s, axis=-1, keepdims=True)
        p = jnp.exp2(s - m)
        m_ref[...] = m
        l_ref[...] = jnp.sum(p, axis=-1, keepdims=True)
        acc_ref[...] = jnp.dot(p.astype(BF16), vn, preferred_element_type=F32)

    for c0 in range(0, len(k_refs), PAGE_CHUNK):
        for g in range(PAGE_CHUNK):
            kbuf[:, g * page:(g + 1) * page] = k_refs[c0 + g][...].astype(BF16)
            for h in range(n_heads):
                vh = v_refs[c0 + g][pl.ds(h, page, stride=n_heads), :]
                vbuf[g * page:(g + 1) * page, h * v_dim:(h + 1) * v_dim] = vh.astype(BF16)
        s = jnp.dot(qbd_ref[...], kbuf[...], preferred_element_type=F32)
        m = m_ref[...]
        m_new = jnp.maximum(m, jnp.max(s, axis=-1, keepdims=True))
        alpha = jnp.exp2(m - m_new)
        p = jnp.exp2(s - m_new)
        l_ref[...] = alpha * l_ref[...] + jnp.sum(p, axis=-1, keepdims=True)
        acc_ref[...] = alpha * acc_ref[...] + jnp.dot(p.astype(BF16), vbuf[...], preferred_element_type=F32)
        m_ref[...] = m_new

    @pl.when(s_idx == n_steps - 1)
    def _():
        o = acc_ref[...] / l_ref[...]
        for h in range(n_heads):
            cols = slice(h * v_dim, (h + 1) * v_dim)
            o1 = o[(2 * h) * tq:(2 * h + 1) * tq, cols]
            o2 = o[(2 * h + 1) * tq:(2 * h + 2) * tq, cols]
            o_ref[:, cols] = _head_out(o1, o2, lam, gsub_ref[...], lam_init)


def _attn_kernel(pt_ref, q_ref, kt_ref, v_ref, qs_ref, kn_ref, vn_ref, *rest,
                 n_pg, steps_per_sample, tq, n_heads, qk_dim, page, lam_init):
    del pt_ref
    k_refs = rest[:n_pg]
    v_refs = rest[n_pg:2 * n_pg]
    (lq1_ref, lk1_ref, lq2_ref, lk2_ref, gsub_ref, o_ref, os_ref,
     qbd_scr, s_scr, p_scr, m_scr, l_scr, a_scr, acc_scr,
     sqbd_scr, kbuf, vbuf, sm_scr, sl_scr, sacc_scr) = rest[2 * n_pg:]
    i = pl.program_id(2)
    step = (pl.program_id(0) * pl.num_programs(1) + pl.program_id(1)) * pl.num_programs(2) + i
    lam = _diff_lambda(lq1_ref, lk1_ref, lq2_ref, lk2_ref, lam_init)
    _sample_attn_body(step % steps_per_sample, steps_per_sample, qs_ref, kn_ref, vn_ref, k_refs, v_refs,
                      lam, gsub_ref, os_ref, sqbd_scr, kbuf, vbuf, sm_scr, sl_scr, sacc_scr,
                      n_heads=n_heads, qk_dim=qk_dim, page=page, lam_init=lam_init)
    _prompt_attn_body(i, q_ref, kt_ref, v_ref, lam, gsub_ref, o_ref,
                      qbd_scr, s_scr, p_scr, m_scr, l_scr, a_scr, acc_scr,
                      tq=tq, qk_dim=qk_dim, lam_init=lam_init, row_chunk=32)


def _attention(q, kt, v, qs, k_new, v_new, kt_pages, v_pages, page_table, lw, *,
               n_heads, qk_dim, lam_init, tq, n_hg):
    batch, seq, feat = q.shape
    dec_batch, dec_seq, _ = qs.shape
    page = kt_pages.shape[-1]
    n_pages = page_table.shape[1]
    v_dim = feat // n_heads
    n_groups = n_heads // n_hg
    n_q = seq // tq
    total_steps = batch * n_groups * n_q
    assert 2 * qk_dim == LANES and v_dim == LANES and page == LANES and n_heads % n_hg == 0
    assert (dec_batch * n_pages) % total_steps == 0
    n_pg = dec_batch * n_pages // total_steps
    assert n_pages % n_pg == 0 and n_pg % PAGE_CHUNK == 0
    steps_per_sample = n_pages // n_pg
    rows = 2 * tq
    srows = 2 * n_heads * dec_seq
    gw = n_hg * LANES

    def linear(b, h, i):
        return (b * n_groups + h) * n_q + i

    small = lambda n: pl.BlockSpec((1, n), lambda b, h, i, pt: (0, 0))
    new_spec = pl.BlockSpec((None, dec_seq, feat), lambda b, h, i, pt: (linear(b, h, i) // steps_per_sample, 0, 0))

    def page_spec(g, shape):
        def index(b, h, i, pt):
            n = linear(b, h, i)
            return (pt[n // steps_per_sample, (n % steps_per_sample) * n_pg + g], 0, 0)
        return pl.BlockSpec((None,) + shape, index)

    kern = functools.partial(_attn_kernel, n_pg=n_pg, steps_per_sample=steps_per_sample, tq=tq,
                             n_heads=n_heads, qk_dim=qk_dim, page=page, lam_init=lam_init)
    grid_spec = pltpu.PrefetchScalarGridSpec(
        num_scalar_prefetch=1,
        grid=(batch, n_groups, n_q),
        in_specs=[
            pl.BlockSpec((None, tq, gw), lambda b, h, i, pt: (b, i, h)),
            pl.BlockSpec((None, gw, seq), lambda b, h, i, pt: (b, h, 0), pipeline_mode=pl.Buffered(1)),
            pl.BlockSpec((None, seq, gw), lambda b, h, i, pt: (b, 0, h), pipeline_mode=pl.Buffered(1)),
            new_spec, new_spec, new_spec]
        + [page_spec(g, (feat, page)) for g in range(n_pg)]
        + [page_spec(g, (page * n_heads, v_dim)) for g in range(n_pg)]
        + [small(qk_dim)] * 4 + [small(v_dim)],
        out_specs=[pl.BlockSpec((None, tq, gw), lambda b, h, i, pt: (b, i, h)), new_spec],
        scratch_shapes=[
            pltpu.VMEM((n_hg, rows, 2 * qk_dim), BF16),
            pltpu.VMEM((n_hg, rows, tq), F32),
            pltpu.VMEM((n_hg, rows, tq), BF16),
            pltpu.VMEM((n_hg, rows, LANES), F32),
            pltpu.VMEM((n_hg, rows, LANES), F32),
            pltpu.VMEM((n_hg, rows, LANES), F32),
            pltpu.VMEM((n_hg, rows, v_dim), F32),
            pltpu.VMEM((srows, feat), BF16),
            pltpu.VMEM((feat, PAGE_CHUNK * page), BF16),
            pltpu.VMEM((PAGE_CHUNK * page, feat), BF16),
            pltpu.VMEM((srows, 1), F32),
            pltpu.VMEM((srows, 1), F32),
            pltpu.VMEM((srows, feat), F32),
        ],
    )
    return pl.pallas_call(
        kern,
        grid_spec=grid_spec,
        out_shape=[jax.ShapeDtypeStruct((batch, seq, feat), BF16),
                   jax.ShapeDtypeStruct((dec_batch, dec_seq, feat), F32)],
        compiler_params=pltpu.CompilerParams(
            dimension_semantics=("arbitrary", "arbitrary", "arbitrary"), vmem_limit_bytes=VMEM_LIMIT),
        name="attention",
    )(page_table, q, kt, v, qs, k_new, v_new, *([kt_pages] * n_pg), *([v_pages] * n_pg),
      lw['lq1'], lw['lk1'], lw['lq2'], lw['lk2'], lw['g_subln'])


def _mix_out_kernel(x_ref, oa_ref, hs_ref, yr_ref, ga_ref, gr_ref, p_ref,
                    wattn_ref, wrec_ref, wout_ref, gffn_ref, wg_ref, wu_ref, wd_ref,
                    gple_ref, wpg_ref, wpp_ref, gfin_ref, y_ref, *, ff_chunk, final_norm):
    dot = functools.partial(jnp.dot, preferred_element_type=F32)
    o_a = dot(oa_ref[...].astype(BF16), wattn_ref[...])
    o_r = dot((hs_ref[...] * jax.nn.gelu(yr_ref[...])).astype(BF16), wrec_ref[...])
    m = jax.nn.sigmoid(ga_ref[...]) * o_a + jax.nn.sigmoid(gr_ref[...]) * o_r
    h = x_ref[...] + dot(m.astype(BF16), wout_ref[...])

    u2 = _rms(h, gffn_ref[...]).astype(BF16)
    d_ff = wg_ref.shape[1]
    ffn = jnp.zeros_like(h)
    for c0 in range(0, d_ff, ff_chunk):
        c1 = min(c0 + ff_chunk, d_ff)
        hid = jax.nn.silu(dot(u2, wg_ref[:, c0:c1])) * dot(u2, wu_ref[:, c0:c1])
        ffn = ffn + dot(hid.astype(BF16), wd_ref[c0:c1, :])
    h = h + ffn

    g = jax.nn.sigmoid(dot(_rms(h, gple_ref[...]).astype(BF16), wpg_ref[...]))
    h = h + g * dot(p_ref[...].astype(BF16), wpp_ref[...])
    y_ref[...] = _rms(h, gfin_ref[...]) if final_norm else h


def _mix_out(x2d, oa, hs, yr, ga, gr, p2d, lw, g_final, *, tm, final_norm):
    n_tok, d_model = x2d.shape
    tok_spec = lambda w: pl.BlockSpec((tm, w), lambda i: (i, 0))
    weights = [lw['w_attn_br'], lw['w_rec_br'], lw['w_out'], lw['g_ffn'], lw['w_ffn_gate'], lw['w_ffn_up'],
               lw['w_ffn_down'], lw['g_ple'], lw['w_ple_gate'], lw['w_ple_proj'], g_final]
    kern = functools.partial(_mix_out_kernel, ff_chunk=1024, final_norm=final_norm)
    return pl.pallas_call(
        kern,
        grid=(n_tok // tm,),
        in_specs=[tok_spec(d_model), tok_spec(oa.shape[1]), tok_spec(hs.shape[1]), tok_spec(yr.shape[1]),
                  tok_spec(d_model), tok_spec(d_model), tok_spec(p2d.shape[1])]
        + [_resident(w.shape) for w in weights],
        out_specs=tok_spec(d_model),
        out_shape=jax.ShapeDtypeStruct((n_tok, d_model), F32),
        compiler_params=pltpu.CompilerParams(
            dimension_semantics=("arbitrary",), vmem_limit_bytes=VMEM_LIMIT),
        name="mix_out",
    )(x2d, oa, hs, yr, ga, gr, p2d, *weights)


def kernel(x_prompt, x_sample, p_prompt, p_sample, cache_k, cache_v, page_table, state_conv, state_h, g_mix, w_in, lambda_q1, lambda_k1, lambda_q2, lambda_k2, g_subln, w_attn_br, w_conv, b_conv, w_gate_a, b_gate_a, w_gate_x, b_gate_x, lru_lambda, w_rec_br, w_out, g_ffn, w_ffn_gate, w_ffn_up, w_ffn_down, g_ple, w_ple_gate, w_ple_proj, g_final):
    depth = w_in.shape[0]
    batch, seq, d_model = x_prompt.shape
    dec_batch, dec_seq, _ = x_sample.shape
    _, n_phys, page, n_heads, _, qk_dim = cache_k.shape
    v_dim = cache_v.shape[-1]
    width = w_conv.shape[-1]
    feat = n_heads * v_dim
    qk_scale = qk_dim ** -0.5 * math.log2(math.e)

    hp = x_prompt.reshape(batch * seq, d_model)
    hs = x_sample.reshape(dec_batch * dec_seq, d_model)
    outs = [[] for _ in range(8)]
    for l in range(depth):
        lam_init = 0.8 - 0.6 * math.exp(-0.3 * l)
        last = l == depth - 1
        lw = dict(
            g_mix=g_mix[l][None], w_in=w_in[l].astype(BF16),
            w_kt=w_in[l][:, feat:2 * feat].T.astype(BF16),
            lq1=lambda_q1[l][None], lk1=lambda_k1[l][None], lq2=lambda_q2[l][None], lk2=lambda_k2[l][None],
            g_subln=g_subln[l][None], w_attn_br=w_attn_br[l].astype(BF16),
            w_conv=w_conv[l], b_conv=b_conv[l][None],
            w_gate=jnp.concatenate([w_gate_a[l], w_gate_x[l]], axis=-1).astype(BF16),
            b_gate=jnp.concatenate([b_gate_a[l], b_gate_x[l]], axis=-1)[:, None, :],
            lru_lambda=lru_lambda[l][None], w_rec_br=w_rec_br[l].astype(BF16), w_out=w_out[l].astype(BF16),
            g_ffn=g_ffn[l][None], w_ffn_gate=w_ffn_gate[l].astype(BF16), w_ffn_up=w_ffn_up[l].astype(BF16),
            w_ffn_down=w_ffn_down[l].astype(BF16), g_ple=g_ple[l][None],
            w_ple_gate=w_ple_gate[l].astype(BF16), w_ple_proj=w_ple_proj[l].astype(BF16))
        gfin = g_final[None]

        conv0 = jnp.zeros((batch, CONV_WIDTH - 1, width), F32)
        h0 = jnp.zeros((batch, 1, width), F32)
        q, kf, vf, kb, vb, rec, yr, ga, gr, cnew, hnew = _mix_in(
            hp, conv0, h0, lw, batch=batch, seq=seq, nb=1, tt=256, qk_scale=qk_scale, k_transposed=True)
        qs, kfs, vfs, _, _, recs, yrs, gas, grs, cnews, hnews = _mix_in(
            hs, state_conv[l], state_h[l][:, None, :], lw,
            batch=dec_batch, seq=dec_seq, nb=dec_batch, tt=dec_seq, qk_scale=qk_scale, k_transposed=False)

        kt_pages = jnp.transpose(cache_k[l], (0, 2, 3, 4, 1)).reshape(n_phys, feat, page)
        v_pages = cache_v[l].reshape(n_phys, page * n_heads, v_dim)
        oa, oas = _attention(
            q.reshape(batch, seq, feat), kb, vb.reshape(batch, seq, feat),
            qs.astype(F32).reshape(dec_batch, dec_seq, feat),
            kfs.reshape(dec_batch, dec_seq, feat), vfs.reshape(dec_batch, dec_seq, feat),
            kt_pages, v_pages, page_table, lw,
            n_heads=n_heads, qk_dim=qk_dim, lam_init=lam_init, tq=512, n_hg=2)

        hp = _mix_out(hp, oa.reshape(batch * seq, feat), rec, yr, ga, gr,
                      p_prompt[l].reshape(batch * seq, -1), lw, gfin, tm=256, final_norm=last)
        hs = _mix_out(hs, oas.reshape(dec_batch * dec_seq, feat), recs, yrs, gas, grs,
                      p_sample[l].reshape(dec_batch * dec_seq, -1), lw, gfin,
                      tm=dec_batch * dec_seq, final_norm=last)

        outs[0].append(jnp.transpose(kf.reshape(batch, n_heads, 2, qk_dim, seq), (0, 4, 1, 2, 3)))
        outs[1].append(vf.reshape(batch, seq, n_heads, v_dim))
        outs[2].append(cnew)
        outs[3].append(hnew.reshape(batch, width))
        outs[4].append(kfs.reshape(dec_batch, dec_seq, n_heads, 2, qk_dim))
        outs[5].append(vfs.reshape(dec_batch, dec_seq, n_heads, v_dim))
        outs[6].append(cnews)
        outs[7].append(hnews.reshape(dec_batch, width))

    y_prompt = hp.reshape(batch, seq, d_model)
    y_sample = hs.reshape(dec_batch, dec_seq, d_model)
    return (y_prompt, y_sample) + tuple(jnp.stack(o) for o in outs)
```

```python
import functools
import math

import jax
import jax.numpy as jnp
from jax import lax
from jax.experimental import pallas as pl
from jax.experimental.pallas import tpu as pltpu

F32 = jnp.float32
BF16 = jnp.bfloat16

EPS = 1e-6
LRU_C = 8.0
CONV_WIDTH = 4
N_LRU_BLOCKS = 8
SUBLANES = 8
LANES = 128
CONV_PAD = 8
PAGE_CHUNK = 8
FILL_COLS = 512
VMEM_LIMIT = 56 * 1024 * 1024


def _rms(x, g):
    ms = jnp.mean(x * x, axis=-1, keepdims=True)
    return x * lax.rsqrt(ms + EPS) * g


def _softplus(x):
    return jnp.maximum(x, 0.0) + jnp.log1p(jnp.exp(-jnp.abs(x)))


def _resident(shape):
    nd = len(shape)
    return pl.BlockSpec(shape, lambda *_: (0,) * nd, pipeline_mode=pl.Buffered(1))


def _mix_in_kernel(x_ref, cprev_ref, hprev_ref, gmix_ref, win_ref, wkt_ref, wconv_ref, bconv_ref,
                   wgate_ref, bgate_ref, lam_ref,
                   q_ref, kf_ref, vf_ref, kb_ref, vb_ref, hs_ref, yr_ref, ga_ref, gr_ref,
                   cnew_ref, hnew_ref,
                   xbuf, abuf, ubuf, hstate, *, nb, tt, width, qk_scale, k_transposed):
    t = pl.program_id(1)
    tm = nb * tt
    lru_block = width // N_LRU_BLOCKS

    @pl.when(t == 0)
    def _():
        xbuf[:, CONV_PAD - (CONV_WIDTH - 1):CONV_PAD, :] = cprev_ref[...]
        hstate[...] = hprev_ref[...]

    u = _rms(x_ref[...], gmix_ref[...]).astype(BF16)

    def proj(i, c0=0, c1=width):
        return jnp.dot(u, win_ref[:, i * width + c0:i * width + c1], preferred_element_type=F32)

    def emit_q(c0, c1):
        q_ref[:, c0:c1] = (proj(0, c0, c1) * qk_scale).astype(BF16)

    def emit_k(c0, c1):
        if k_transposed:
            k = lax.dot_general(wkt_ref[c0:c1, :], u, (((1,), (1,)), ((), ())), preferred_element_type=F32)
            kf_ref[c0:c1, :] = k
            kb_ref[c0:c1, :] = k.astype(BF16)
        else:
            k = proj(1, c0, c1)
            kf_ref[:, c0:c1] = k
            kb_ref[:, c0:c1] = k.astype(BF16)

    def emit_v(c0, c1):
        v = proj(2, c0, c1)
        vf_ref[:, c0:c1] = v
        vb_ref[:, c0:c1] = v.astype(BF16)

    def emit_raw(i, ref, c0, c1):
        ref[:, c0:c1] = proj(i, c0, c1)

    fillers = []
    for c0 in range(0, width, FILL_COLS):
        c1 = c0 + FILL_COLS
        fillers += [functools.partial(emit_q, c0, c1), functools.partial(emit_k, c0, c1),
                    functools.partial(emit_v, c0, c1), functools.partial(emit_raw, 4, yr_ref, c0, c1),
                    functools.partial(emit_raw, 5, ga_ref, c0, c1), functools.partial(emit_raw, 6, gr_ref, c0, c1)]
    w_conv, w_gate, w_scan = 4, 6, 1
    total_weight = w_conv + N_LRU_BLOCKS * w_gate + nb * (tt // SUBLANES) * w_scan
    done = [0, 0]

    def fill(weight):
        done[0] += weight
        target = -(-len(fillers) * done[0] // total_weight)
        while done[1] < target:
            fillers[done[1]]()
            done[1] += 1

    xbuf[:, CONV_PAD:CONV_PAD + tt, :] = proj(3).reshape(nb, tt, width)
    fill(w_conv)
    xc = bconv_ref[...].reshape(1, 1, width)
    for j in range(CONV_WIDTH):
        r0 = CONV_PAD - (CONV_WIDTH - 1) + j
        xc = xc + xbuf[:, r0:r0 + tt, :] * wconv_ref[j:j + 1, :].reshape(1, 1, width)
    tail = xbuf[:, CONV_PAD + tt - (CONV_WIDTH - 1):CONV_PAD + tt, :]
    cnew_ref[...] = tail
    xbuf[:, CONV_PAD - (CONV_WIDTH - 1):CONV_PAD, :] = tail

    xc = xc.reshape(tm, width)
    xcb = xc.astype(BF16)
    sp = _softplus(-lam_ref[...])
    for blk in range(N_LRU_BLOCKS):
        fill(w_gate)
        sl = slice(blk * lru_block, (blk + 1) * lru_block)
        g = jnp.dot(xcb[:, sl], wgate_ref[blk], preferred_element_type=F32) + bgate_ref[blk]
        r = jax.nn.sigmoid(g[:, :lru_block])
        ig = jax.nn.sigmoid(g[:, lru_block:])
        a = jnp.exp(-LRU_C * r * sp[:, sl])
        abuf[:, sl] = a
        ubuf[:, sl] = jnp.sqrt(1.0 - a * a) * (ig * xc[:, sl])

    n_groups = tt // SUBLANES
    row = lax.broadcasted_iota(jnp.int32, (SUBLANES, width), 0)

    for b in range(nb):
        h = hstate[b]
        for g in range(n_groups):
            fill(w_scan)
            r0 = (b * n_groups + g) * SUBLANES
            a = abuf[r0:r0 + SUBLANES, :]
            uu = ubuf[r0:r0 + SUBLANES, :]
            d = 1
            while d < SUBLANES:
                a_sh = jnp.where(row >= d, pltpu.roll(a, d, 0), 1.0)
                u_sh = jnp.where(row >= d, pltpu.roll(uu, d, 0), 0.0)
                uu = a * u_sh + uu
                a = a * a_sh
                d *= 2
            hb = uu + a * h
            hs_ref[r0:r0 + SUBLANES, :] = hb
            h = hb[SUBLANES - 1:SUBLANES, :]
        hstate[b] = h
    hnew_ref[...] = hstate[...]
    assert done[1] == len(fillers)


def _mix_in(x2d, conv_prev, h_prev, lw, *, batch, seq, nb, tt, qk_scale, k_transposed):
    n_tok, d_model = x2d.shape
    width = lw['w_conv'].shape[-1]
    n_t = seq // tt
    tm = nb * tt
    grid = (batch // nb, n_t)
    tok = lambda dt: jax.ShapeDtypeStruct((n_tok, width), dt)
    tok_spec = pl.BlockSpec((tm, width), lambda b, t: (b * n_t + t, 0))
    if k_transposed:
        assert nb == 1
        k_shape = lambda dt: jax.ShapeDtypeStruct((batch, width, seq), dt)
        k_spec = pl.BlockSpec((None, width, tt), lambda b, t: (b, 0, t))
    else:
        k_shape, k_spec = tok, tok_spec
    state_c = pl.BlockSpec((nb, CONV_WIDTH - 1, width), lambda b, t: (b, 0, 0))
    state_h = pl.BlockSpec((nb, 1, width), lambda b, t: (b, 0, 0))
    kern = functools.partial(_mix_in_kernel, nb=nb, tt=tt, width=width, qk_scale=qk_scale,
                             k_transposed=k_transposed)
    return pl.pallas_call(
        kern,
        grid=grid,
        in_specs=[
            pl.BlockSpec((tm, d_model), lambda b, t: (b * n_t + t, 0)),
            state_c, state_h,
            _resident((1, d_model)),
            _resident(lw['w_in'].shape),
            _resident(lw['w_kt'].shape),
            _resident((CONV_WIDTH, width)),
            _resident((1, width)),
            _resident(lw['w_gate'].shape),
            _resident(lw['b_gate'].shape),
            _resident((1, width)),
        ],
        out_specs=[tok_spec, k_spec, tok_spec, k_spec] + [tok_spec] * 5 + [state_c, state_h],
        out_shape=[tok(BF16), k_shape(F32), tok(F32), k_shape(BF16), tok(BF16),
                   tok(F32), tok(F32), tok(F32), tok(F32),
                   jax.ShapeDtypeStruct((batch, CONV_WIDTH - 1, width), F32),
                   jax.ShapeDtypeStruct((batch, 1, width), F32)],
        scratch_shapes=[
            pltpu.VMEM((nb, CONV_PAD + tt, width), F32),
            pltpu.VMEM((tm, width), F32),
            pltpu.VMEM((tm, width), F32),
            pltpu.VMEM((nb, 1, width), F32),
        ],
        compiler_params=pltpu.CompilerParams(
            dimension_semantics=("arbitrary", "arbitrary"), vmem_limit_bytes=VMEM_LIMIT),
        name="mix_in",
    )(x2d, conv_prev, h_prev, lw['g_mix'], lw['w_in'], lw['w_kt'], lw['w_conv'], lw['b_conv'],
      lw['w_gate'], lw['b_gate'], lw['lru_lambda'])


def _diff_lambda(lq1_ref, lk1_ref, lq2_ref, lk2_ref, lam_init):
    s1 = jnp.sum(lq1_ref[...] * lk1_ref[...], axis=-1, keepdims=True)
    s2 = jnp.sum(lq2_ref[...] * lk2_ref[...], axis=-1, keepdims=True)
    return jnp.exp(s1) - jnp.exp(s2) + lam_init


def _head_out(o1, o2, lam, gsub, lam_init):
    od = o1 - lam * o2
    return _rms(od, gsub) * (1.0 - lam_init)


def _prompt_attn_body(i, q_ref, kt_ref, v_ref, lam, gsub_ref, o_ref,
                      qbd_scr, s_scr, p_scr, m_scr, l_scr, a_scr, acc_scr, *, tq, qk_dim, lam_init, row_chunk):
    tk = tq
    rows = 2 * tq
    n_hg = qbd_scr.shape[0]
    lane = lax.broadcasted_iota(jnp.int32, (tq, LANES), 1)
    for hh in range(n_hg):
        q = q_ref[:, hh * LANES:(hh + 1) * LANES]
        zero = jnp.zeros_like(q)
        qbd_scr[hh, 0:tq, :] = jnp.where(lane < qk_dim, q, zero)
        qbd_scr[hh, tq:rows, :] = jnp.where(lane >= qk_dim, q, zero)
    m_scr[...] = jnp.full(m_scr.shape, -jnp.inf, F32)
    l_scr[...] = jnp.zeros(l_scr.shape, F32)
    acc_scr[...] = jnp.zeros(acc_scr.shape, F32)

    def head_step(hh, k0, masked):
        hs = slice(hh * LANES, (hh + 1) * LANES)
        s_scr[hh] = jnp.dot(qbd_scr[hh], kt_ref[hs, pl.ds(k0, tk)], preferred_element_type=F32)
        for c in range(rows // row_chunk):
            r0 = c * row_chunk
            rs = slice(r0, r0 + row_chunk)
            s = s_scr[hh, rs, :]
            if masked:
                qpos = lax.broadcasted_iota(jnp.int32, (row_chunk, tk), 0) + (r0 % tq)
                kpos = lax.broadcasted_iota(jnp.int32, (row_chunk, tk), 1)
                s = jnp.where(kpos <= qpos, s, -jnp.inf)
            tiles = [s[:, t * LANES:(t + 1) * LANES] for t in range(tk // LANES)]
            mx = functools.reduce(jnp.maximum, tiles)
            m_old = m_scr[hh, rs, :]
            m_new = jnp.maximum(m_old, jnp.max(mx, axis=-1, keepdims=True))
            alpha = jnp.exp2(m_old - m_new)
            ps = [jnp.exp2(t - m_new) for t in tiles]
            l_scr[hh, rs, :] = alpha * l_scr[hh, rs, :] + functools.reduce(jnp.add, ps)
            m_scr[hh, rs, :] = m_new
            a_scr[hh, rs, :] = alpha
            p_scr[hh, rs, :] = jnp.concatenate(ps, axis=1).astype(BF16)
        pv = jnp.dot(p_scr[hh], v_ref[pl.ds(k0, tk), hs], preferred_element_type=F32)
        acc_scr[hh] = a_scr[hh] * acc_scr[hh] + pv

    def step(j, masked):
        k0 = pl.multiple_of(j * tk, tk)
        for hh in range(n_hg):
            head_step(hh, k0, masked)

    def unmasked(j, carry):
        step(j, False)
        return carry

    lax.fori_loop(0, i, unmasked, 0)
    step(i, True)
    for hh in range(n_hg):
        o = acc_scr[hh] / jnp.sum(l_scr[hh], axis=-1, keepdims=True)
        o_ref[:, hh * LANES:(hh + 1) * LANES] = _head_out(
            o[:tq], o[tq:], lam, gsub_ref[...], lam_init).astype(o_ref.dtype)


def _sample_attn_body(s_idx, n_steps, q_ref, kn_ref, vn_ref, k_refs, v_refs, lam, gsub_ref, o_ref,
                      qbd_ref, kbuf, vbuf, m_ref, l_ref, acc_ref, *, n_heads, qk_dim, page, lam_init):
    tq, feat = q_ref.shape
    n_maps = 2 * n_heads
    rows = n_maps * tq
    v_dim = feat // n_heads

    @pl.when(s_idx == 0)
    def _():
        qt = jnp.concatenate([q_ref[...]] * n_maps, axis=0)
        r = lax.broadcasted_iota(jnp.int32, (rows, feat), 0)
        c = lax.broadcasted_iota(jnp.int32, (rows, feat), 1)
        qbd = jnp.where(r // tq == c // qk_dim, qt, 0.0).astype(BF16)
        qbd_ref[...] = qbd
        pad = jnp.zeros((page - tq, feat), F32)
        kn = jnp.concatenate([kn_ref[...], pad], axis=0).astype(BF16)
        vn = jnp.concatenate([vn_ref[...], pad], axis=0).astype(BF16)
        s = lax.dot_general(qbd, kn, (((1,), (1,)), ((), ())), preferred_element_type=F32)
        rr = lax.broadcasted_iota(jnp.int32, (rows, page), 0)
        cc = lax.broadcasted_iota(jnp.int32, (rows, page), 1)
        s = jnp.where(cc <= rr % tq, s, -jnp.inf)
        m = jnp.max(s, axis=-1, keepdims=True)
        p = jnp.exp2(s - m)
        m_ref[...] = m
        l_ref[...] = jnp.sum(p, axis=-1, keepdims=True)
        acc_ref[...] = jnp.dot(p.astype(BF16), vn, preferred_element_type=F32)

    for c0 in range(0, len(k_refs), PAGE_CHUNK):
        for g in range(PAGE_CHUNK):
            kbuf[:, g * page:(g + 1) * page] = k_refs[c0 + g][...].astype(BF16)
            for h in range(n_heads):
                vh = v_refs[c0 + g][pl.ds(h, page, stride=n_heads), :]
                vbuf[g * page:(g + 1) * page, h * v_dim:(h + 1) * v_dim] = vh.astype(BF16)
        s = jnp.dot(qbd_ref[...], kbuf[...], preferred_element_type=F32)
        m = m_ref[...]
        m_new = jnp.maximum(m, jnp.max(s, axis=-1, keepdims=True))
        alpha = jnp.exp2(m - m_new)
        p = jnp.exp2(s - m_new)
        l_ref[...] = alpha * l_ref[...] + jnp.sum(p, axis=-1, keepdims=True)
        acc_ref[...] = alpha * acc_ref[...] + jnp.dot(p.astype(BF16), vbuf[...], preferred_element_type=F32)
        m_ref[...] = m_new

    @pl.when(s_idx == n_steps - 1)
    def _():
        o = acc_ref[...] / l_ref[...]
        for h in range(n_heads):
            cols = slice(h * v_dim, (h + 1) * v_dim)
            o1 = o[(2 * h) * tq:(2 * h + 1) * tq, cols]
            o2 = o[(2 * h + 1) * tq:(2 * h + 2) * tq, cols]
            o_ref[:, cols] = _head_out(o1, o2, lam, gsub_ref[...], lam_init)


def _page_copies(pt_ref, kt_hbm, v_hbm, kpg, vpg, sem, step, slot, *, n_pg, steps_per_sample):
    batch_idx = step // steps_per_sample
    first = (step % steps_per_sample) * n_pg
    copies = []
    for g in range(n_pg):
        pid = pt_ref[batch_idx, first + g]
        copies.append(pltpu.make_async_copy(kt_hbm.at[pid], kpg.at[slot, g], sem.at[slot]))
        copies.append(pltpu.make_async_copy(v_hbm.at[pid], vpg.at[slot, g], sem.at[slot]))
    return copies


def _attn_kernel(pt_ref, q_ref, kt_ref, v_ref, qs_ref, kn_ref, vn_ref, kt_hbm, v_hbm,
                 lq1_ref, lk1_ref, lq2_ref, lk2_ref, gsub_ref, o_ref, os_ref,
                 qbd_scr, s_scr, p_scr, m_scr, l_scr, a_scr, acc_scr,
                 sqbd_scr, kbuf, vbuf, sm_scr, sl_scr, sacc_scr, kpg, vpg, page_sem,
                 *, n_pg, steps_per_sample, tq, n_heads, qk_dim, page, lam_init):
    i = pl.program_id(2)
    step = (pl.program_id(0) * pl.num_programs(1) + pl.program_id(1)) * pl.num_programs(2) + i
    total_steps = pl.num_programs(0) * pl.num_programs(1) * pl.num_programs(2)
    slot = step % 2
    copies = functools.partial(_page_copies, pt_ref, kt_hbm, v_hbm, kpg, vpg, page_sem,
                               n_pg=n_pg, steps_per_sample=steps_per_sample)

    @pl.when(step == 0)
    def _():
        for c in copies(step, slot):
            c.start()

    @pl.when(step + 1 < total_steps)
    def _():
        for c in copies(step + 1, 1 - slot):
            c.start()

    for c in copies(step, slot):
        c.wait()
    k_refs = [kpg.at[slot, g] for g in range(n_pg)]
    v_refs = [vpg.at[slot, g] for g in range(n_pg)]
    lam = _diff_lambda(lq1_ref, lk1_ref, lq2_ref, lk2_ref, lam_init)
    _sample_attn_body(step % steps_per_sample, steps_per_sample, qs_ref, kn_ref, vn_ref, k_refs, v_refs,
                      lam, gsub_ref, os_ref, sqbd_scr, kbuf, vbuf, sm_scr, sl_scr, sacc_scr,
                      n_heads=n_heads, qk_dim=qk_dim, page=page, lam_init=lam_init)
    _prompt_attn_body(i, q_ref, kt_ref, v_ref, lam, gsub_ref, o_ref,
                      qbd_scr, s_scr, p_scr, m_scr, l_scr, a_scr, acc_scr,
                      tq=tq, qk_dim=qk_dim, lam_init=lam_init, row_chunk=32)


def _attention(q, kt, v, qs, k_new, v_new, kt_pages, v_pages, page_table, lw, *,
               n_heads, qk_dim, lam_init, tq, n_hg):
    batch, seq, feat = q.shape
    dec_batch, dec_seq, _ = qs.shape
    page = kt_pages.shape[-1]
    n_pages = page_table.shape[1]
    v_dim = feat // n_heads
    n_groups = n_heads // n_hg
    n_q = seq // tq
    total_steps = batch * n_groups * n_q
    assert 2 * qk_dim == LANES and v_dim == LANES and page == LANES and n_heads % n_hg == 0
    assert (dec_batch * n_pages) % total_steps == 0
    n_pg = dec_batch * n_pages // total_steps
    assert n_pages % n_pg == 0 and n_pg % PAGE_CHUNK == 0
    steps_per_sample = n_pages // n_pg
    rows = 2 * tq
    srows = 2 * n_heads * dec_seq
    gw = n_hg * LANES

    def linear(b, h, i):
        return (b * n_groups + h) * n_q + i

    small = lambda n: pl.BlockSpec((1, n), lambda b, h, i, pt: (0, 0))
    new_spec = pl.BlockSpec((None, dec_seq, feat), lambda b, h, i, pt: (linear(b, h, i) // steps_per_sample, 0, 0))

    kern = functools.partial(_attn_kernel, n_pg=n_pg, steps_per_sample=steps_per_sample, tq=tq,
                             n_heads=n_heads, qk_dim=qk_dim, page=page, lam_init=lam_init)
    grid_spec = pltpu.PrefetchScalarGridSpec(
        num_scalar_prefetch=1,
        grid=(batch, n_groups, n_q),
        in_specs=[
            pl.BlockSpec((None, tq, gw), lambda b, h, i, pt: (b, i, h)),
            pl.BlockSpec((None, gw, seq), lambda b, h, i, pt: (b, h, 0), pipeline_mode=pl.Buffered(1)),
            pl.BlockSpec((None, seq, gw), lambda b, h, i, pt: (b, 0, h), pipeline_mode=pl.Buffered(1)),
            new_spec, new_spec, new_spec,
            pl.BlockSpec(memory_space=pl.ANY), pl.BlockSpec(memory_space=pl.ANY)]
        + [small(qk_dim)] * 4 + [small(v_dim)],
        out_specs=[pl.BlockSpec((None, tq, gw), lambda b, h, i, pt: (b, i, h)), new_spec],
        scratch_shapes=[
            pltpu.VMEM((n_hg, rows, 2 * qk_dim), BF16),
            pltpu.VMEM((n_hg, rows, tq), F32),
            pltpu.VMEM((n_hg, rows, tq), BF16),
            pltpu.VMEM((n_hg, rows, LANES), F32),
            pltpu.VMEM((n_hg, rows, LANES), F32),
            pltpu.VMEM((n_hg, rows, LANES), F32),
            pltpu.VMEM((n_hg, rows, v_dim), F32),
            pltpu.VMEM((srows, feat), BF16),
            pltpu.VMEM((feat, PAGE_CHUNK * page), BF16),
            pltpu.VMEM((PAGE_CHUNK * page, feat), BF16),
            pltpu.VMEM((srows, 1), F32),
            pltpu.VMEM((srows, 1), F32),
            pltpu.VMEM((srows, feat), F32),
            pltpu.VMEM((2, n_pg, feat, page), F32),
            pltpu.VMEM((2, n_pg, page * n_heads, v_dim), F32),
            pltpu.SemaphoreType.DMA((2,)),
        ],
    )
    return pl.pallas_call(
        kern,
        grid_spec=grid_spec,
        out_shape=[jax.ShapeDtypeStruct((batch, seq, feat), BF16),
                   jax.ShapeDtypeStruct((dec_batch, dec_seq, feat), F32)],
        compiler_params=pltpu.CompilerParams(
            dimension_semantics=("arbitrary", "arbitrary", "arbitrary"), vmem_limit_bytes=VMEM_LIMIT),
        name="attention",
    )(page_table, q, kt, v, qs, k_new, v_new, kt_pages, v_pages,
      lw['lq1'], lw['lk1'], lw['lq2'], lw['lk2'], lw['g_subln'])


def _mix_out_kernel(x_ref, oa_ref, hs_ref, yr_ref, ga_ref, gr_ref, p_ref,
                    wattn_ref, wrec_ref, wout_ref, gffn_ref, wg_ref, wu_ref, wd_ref,
                    gple_ref, wpg_ref, wpp_ref, gfin_ref, y_ref, *, ff_chunk, final_norm):
    dot = functools.partial(jnp.dot, preferred_element_type=F32)
    o_a = dot(oa_ref[...].astype(BF16), wattn_ref[...])
    o_r = dot((hs_ref[...] * jax.nn.gelu(yr_ref[...])).astype(BF16), wrec_ref[...])
    m = jax.nn.sigmoid(ga_ref[...]) * o_a + jax.nn.sigmoid(gr_ref[...]) * o_r
    h = x_ref[...] + dot(m.astype(BF16), wout_ref[...])

    u2 = _rms(h, gffn_ref[...]).astype(BF16)
    d_ff = wg_ref.shape[1]
    ffn = jnp.zeros_like(h)
    for c0 in range(0, d_ff, ff_chunk):
        c1 = min(c0 + ff_chunk, d_ff)
        hid = jax.nn.silu(dot(u2, wg_ref[:, c0:c1])) * dot(u2, wu_ref[:, c0:c1])
        ffn = ffn + dot(hid.astype(BF16), wd_ref[c0:c1, :])
    h = h + ffn

    g = jax.nn.sigmoid(dot(_rms(h, gple_ref[...]).astype(BF16), wpg_ref[...]))
    h = h + g * dot(p_ref[...].astype(BF16), wpp_ref[...])
    y_ref[...] = _rms(h, gfin_ref[...]) if final_norm else h


def _mix_out(x2d, oa, hs, yr, ga, gr, p2d, lw, g_final, *, tm, final_norm):
    n_tok, d_model = x2d.shape
    tok_spec = lambda w: pl.BlockSpec((tm, w), lambda i: (i, 0))
    weights = [lw['w_attn_br'], lw['w_rec_br'], lw['w_out'], lw['g_ffn'], lw['w_ffn_gate'], lw['w_ffn_up'],
               lw['w_ffn_down'], lw['g_ple'], lw['w_ple_gate'], lw['w_ple_proj'], g_final]
    kern = functools.partial(_mix_out_kernel, ff_chunk=1024, final_norm=final_norm)
    return pl.pallas_call(
        kern,
        grid=(n_tok // tm,),
        in_specs=[tok_spec(d_model), tok_spec(oa.shape[1]), tok_spec(hs.shape[1]), tok_spec(yr.shape[1]),
                  tok_spec(d_model), tok_spec(d_model), tok_spec(p2d.shape[1])]
        + [_resident(w.shape) for w in weights],
        out_specs=tok_spec(d_model),
        out_shape=jax.ShapeDtypeStruct((n_tok, d_model), F32),
        compiler_params=pltpu.CompilerParams(
            dimension_semantics=("arbitrary",), vmem_limit_bytes=VMEM_LIMIT),
        name="mix_out",
    )(x2d, oa, hs, yr, ga, gr, p2d, *weights)


def kernel(x_prompt, x_sample, p_prompt, p_sample, cache_k, cache_v, page_table, state_conv, state_h, g_mix, w_in, lambda_q1, lambda_k1, lambda_q2, lambda_k2, g_subln, w_attn_br, w_conv, b_conv, w_gate_a, b_gate_a, w_gate_x, b_gate_x, lru_lambda, w_rec_br, w_out, g_ffn, w_ffn_gate, w_ffn_up, w_ffn_down, g_ple, w_ple_gate, w_ple_proj, g_final):
    depth = w_in.shape[0]
    batch, seq, d_model = x_prompt.shape
    dec_batch, dec_seq, _ = x_sample.shape
    _, n_phys, page, n_heads, _, qk_dim = cache_k.shape
    v_dim = cache_v.shape[-1]
    width = w_conv.shape[-1]
    feat = n_heads * v_dim
    qk_scale = qk_dim ** -0.5 * math.log2(math.e)

    hp = x_prompt.reshape(batch * seq, d_model)
    hs = x_sample.reshape(dec_batch * dec_seq, d_model)
    outs = [[] for _ in range(8)]
    for l in range(depth):
        lam_init = 0.8 - 0.6 * math.exp(-0.3 * l)
        last = l == depth - 1
        lw = dict(
            g_mix=g_mix[l][None], w_in=w_in[l].astype(BF16),
            w_kt=w_in[l][:, feat:2 * feat].T.astype(BF16),
            lq1=lambda_q1[l][None], lk1=lambda_k1[l][None], lq2=lambda_q2[l][None], lk2=lambda_k2[l][None],
            g_subln=g_subln[l][None], w_attn_br=w_attn_br[l].astype(BF16),
            w_conv=w_conv[l], b_conv=b_conv[l][None],
            w_gate=jnp.concatenate([w_gate_a[l], w_gate_x[l]], axis=-1).astype(BF16),
            b_gate=jnp.concatenate([b_gate_a[l], b_gate_x[l]], axis=-1)[:, None, :],
            lru_lambda=lru_lambda[l][None], w_rec_br=w_rec_br[l].astype(BF16), w_out=w_out[l].astype(BF16),
            g_ffn=g_ffn[l][None], w_ffn_gate=w_ffn_gate[l].astype(BF16), w_ffn_up=w_ffn_up[l].astype(BF16),
            w_ffn_down=w_ffn_down[l].astype(BF16), g_ple=g_ple[l][None],
            w_ple_gate=w_ple_gate[l].astype(BF16), w_ple_proj=w_ple_proj[l].astype(BF16))
        gfin = g_final[None]

        conv0 = jnp.zeros((batch, CONV_WIDTH - 1, width), F32)
        h0 = jnp.zeros((batch, 1, width), F32)
        q, kf, vf, kb, vb, rec, yr, ga, gr, cnew, hnew = _mix_in(
            hp, conv0, h0, lw, batch=batch, seq=seq, nb=1, tt=256, qk_scale=qk_scale, k_transposed=True)
        qs, kfs, vfs, _, _, recs, yrs, gas, grs, cnews, hnews = _mix_in(
            hs, state_conv[l], state_h[l][:, None, :], lw,
            batch=dec_batch, seq=dec_seq, nb=dec_batch, tt=dec_seq, qk_scale=qk_scale, k_transposed=False)

        kt_pages = jnp.transpose(cache_k[l], (0, 2, 3, 4, 1)).reshape(n_phys, feat, page)
        v_pages = cache_v[l].reshape(n_phys, page * n_heads, v_dim)
        oa, oas = _attention(
            q.reshape(batch, seq, feat), kb, vb.reshape(batch, seq, feat),
            qs.astype(F32).reshape(dec_batch, dec_seq, feat),
            kfs.reshape(dec_batch, dec_seq, feat), vfs.reshape(dec_batch, dec_seq, feat),
            kt_pages, v_pages, page_table, lw,
            n_heads=n_heads, qk_dim=qk_dim, lam_init=lam_init, tq=512, n_hg=2)

        hp = _mix_out(hp, oa.reshape(batch * seq, feat), rec, yr, ga, gr,
                      p_prompt[l].reshape(batch * seq, -1), lw, gfin, tm=256, final_norm=last)
        hs = _mix_out(hs, oas.reshape(dec_batch * dec_seq, feat), recs, yrs, gas, grs,
                      p_sample[l].reshape(dec_batch * dec_seq, -1), lw, gfin,
                      tm=dec_batch * dec_seq, final_norm=last)

        outs[0].append(jnp.transpose(kf.reshape(batch, n_heads, 2, qk_dim, seq), (0, 4, 1, 2, 3)))
        outs[1].append(vf.reshape(batch, seq, n_heads, v_dim))
        outs[2].append(cnew)
        outs[3].append(hnew.reshape(batch, width))
        outs[4].append(kfs.reshape(dec_batch, dec_seq, n_heads, 2, qk_dim))
        outs[5].append(vfs.reshape(dec_batch, dec_seq, n_heads, v_dim))
        outs[6].append(cnews)
        outs[7].append(hnews.reshape(dec_batch, width))

    y_prompt = hp.reshape(batch, seq, d_model)
    y_sample = hs.reshape(dec_batch, dec_seq, d_model)
    return (y_prompt, y_sample) + tuple(jnp.stack(o) for o in outs)
```

```python
import functools
import math

import jax
import jax.numpy as jnp
from jax import lax
from jax.experimental import pallas as pl
from jax.experimental.pallas import tpu as pltpu

F32 = jnp.float32
BF16 = jnp.bfloat16

EPS = 1e-6
LRU_C = 8.0
CONV_WIDTH = 4
N_LRU_BLOCKS = 8
SUBLANES = 8
LANES = 128
CONV_PAD = 8
PAGE_CHUNK = 8
FILL_COLS = 512
VMEM_LIMIT = 60 * 1024 * 1024


def _rms(x, g):
    ms = jnp.mean(x * x, axis=-1, keepdims=True)
    return x * lax.rsqrt(ms + EPS) * g


def _softplus(x):
    return jnp.maximum(x, 0.0) + jnp.log1p(jnp.exp(-jnp.abs(x)))


def _resident(shape):
    nd = len(shape)
    return pl.BlockSpec(shape, lambda *_: (0,) * nd, pipeline_mode=pl.Buffered(1))


def _mix_in_kernel(x_ref, cprev_ref, hprev_ref, gmix_ref, win_ref, wkt_ref, wconv_ref, bconv_ref,
                   wgate_ref, bgate_ref, lam_ref,
                   q_ref, kf_ref, vf_ref, kb_ref, vb_ref, hs_ref, yr_ref, ga_ref, gr_ref,
                   cnew_ref, hnew_ref,
                   xbuf, abuf, ubuf, hstate, *, nb, tt, width, qk_scale, k_transposed):
    t = pl.program_id(1)
    tm = nb * tt
    lru_block = width // N_LRU_BLOCKS

    @pl.when(t == 0)
    def _():
        xbuf[:, CONV_PAD - (CONV_WIDTH - 1):CONV_PAD, :] = cprev_ref[...]
        hstate[...] = hprev_ref[...]

    u = _rms(x_ref[...], gmix_ref[...]).astype(BF16)

    def proj(i, c0=0, c1=width):
        return jnp.dot(u, win_ref[:, i * width + c0:i * width + c1], preferred_element_type=F32)

    def emit_q(c0, c1):
        q_ref[:, c0:c1] = (proj(0, c0, c1) * qk_scale).astype(BF16)

    def emit_k(c0, c1):
        if k_transposed:
            k = lax.dot_general(wkt_ref[c0:c1, :], u, (((1,), (1,)), ((), ())), preferred_element_type=F32)
            kf_ref[c0:c1, :] = k
            kb_ref[c0:c1, :] = k.astype(BF16)
        else:
            k = proj(1, c0, c1)
            kf_ref[:, c0:c1] = k
            kb_ref[:, c0:c1] = k.astype(BF16)

    def emit_v(c0, c1):
        v = proj(2, c0, c1)
        vf_ref[:, c0:c1] = v
        vb_ref[:, c0:c1] = v.astype(BF16)

    def emit_raw(i, ref, c0, c1):
        ref[:, c0:c1] = proj(i, c0, c1)

    fillers = []
    for c0 in range(0, width, FILL_COLS):
        c1 = c0 + FILL_COLS
        fillers += [functools.partial(emit_q, c0, c1), functools.partial(emit_k, c0, c1),
                    functools.partial(emit_v, c0, c1), functools.partial(emit_raw, 4, yr_ref, c0, c1),
                    functools.partial(emit_raw, 5, ga_ref, c0, c1), functools.partial(emit_raw, 6, gr_ref, c0, c1)]
    w_conv, w_gate, w_scan = 4, 6, 1
    total_weight = w_conv + N_LRU_BLOCKS * w_gate + nb * (tt // SUBLANES) * w_scan
    done = [0, 0]

    def fill(weight):
        done[0] += weight
        target = -(-len(fillers) * done[0] // total_weight)
        while done[1] < target:
            fillers[done[1]]()
            done[1] += 1

    xbuf[:, CONV_PAD:CONV_PAD + tt, :] = proj(3).reshape(nb, tt, width)
    fill(w_conv)
    xc = bconv_ref[...].reshape(1, 1, width)
    for j in range(CONV_WIDTH):
        r0 = CONV_PAD - (CONV_WIDTH - 1) + j
        xc = xc + xbuf[:, r0:r0 + tt, :] * wconv_ref[j:j + 1, :].reshape(1, 1, width)
    tail = xbuf[:, CONV_PAD + tt - (CONV_WIDTH - 1):CONV_PAD + tt, :]
    cnew_ref[...] = tail
    xbuf[:, CONV_PAD - (CONV_WIDTH - 1):CONV_PAD, :] = tail

    xc = xc.reshape(tm, width)
    xcb = xc.astype(BF16)
    sp = _softplus(-lam_ref[...])
    for blk in range(N_LRU_BLOCKS):
        fill(w_gate)
        sl = slice(blk * lru_block, (blk + 1) * lru_block)
        g = jnp.dot(xcb[:, sl], wgate_ref[blk], preferred_element_type=F32) + bgate_ref[blk]
        r = jax.nn.sigmoid(g[:, :lru_block])
        ig = jax.nn.sigmoid(g[:, lru_block:])
        a = jnp.exp(-LRU_C * r * sp[:, sl])
        abuf[:, sl] = a
        ubuf[:, sl] = jnp.sqrt(1.0 - a * a) * (ig * xc[:, sl])

    n_groups = tt // SUBLANES
    row = lax.broadcasted_iota(jnp.int32, (SUBLANES, width), 0)

    for b in range(nb):
        h = hstate[b]
        for g in range(n_groups):
            fill(w_scan)
            r0 = (b * n_groups + g) * SUBLANES
            a = abuf[r0:r0 + SUBLANES, :]
            uu = ubuf[r0:r0 + SUBLANES, :]
            d = 1
            while d < SUBLANES:
                a_sh = jnp.where(row >= d, pltpu.roll(a, d, 0), 1.0)
                u_sh = jnp.where(row >= d, pltpu.roll(uu, d, 0), 0.0)
                uu = a * u_sh + uu
                a = a * a_sh
                d *= 2
            hb = uu + a * h
            hs_ref[r0:r0 + SUBLANES, :] = hb
            h = hb[SUBLANES - 1:SUBLANES, :]
        hstate[b] = h
    hnew_ref[...] = hstate[...]
    assert done[1] == len(fillers)


def _mix_in(x2d, conv_prev, h_prev, lw, *, batch, seq, nb, tt, qk_scale, k_transposed):
    n_tok, d_model = x2d.shape
    width = lw['w_conv'].shape[-1]
    n_t = seq // tt
    tm = nb * tt
    grid = (batch // nb, n_t)
    tok = lambda dt: jax.ShapeDtypeStruct((n_tok, width), dt)
    tok_spec = pl.BlockSpec((tm, width), lambda b, t: (b * n_t + t, 0))
    if k_transposed:
        assert nb == 1
        k_shape = lambda dt: jax.ShapeDtypeStruct((batch, width, seq), dt)
        k_spec = pl.BlockSpec((None, width, tt), lambda b, t: (b, 0, t))
    else:
        k_shape, k_spec = tok, tok_spec
    state_c = pl.BlockSpec((nb, CONV_WIDTH - 1, width), lambda b, t: (b, 0, 0))
    state_h = pl.BlockSpec((nb, 1, width), lambda b, t: (b, 0, 0))
    kern = functools.partial(_mix_in_kernel, nb=nb, tt=tt, width=width, qk_scale=qk_scale,
                             k_transposed=k_transposed)
    return pl.pallas_call(
        kern,
        grid=grid,
        in_specs=[
            pl.BlockSpec((tm, d_model), lambda b, t: (b * n_t + t, 0)),
            state_c, state_h,
            _resident((1, d_model)),
            _resident(lw['w_in'].shape),
            _resident(lw['w_kt'].shape),
            _resident((CONV_WIDTH, width)),
            _resident((1, width)),
            _resident(lw['w_gate'].shape),
            _resident(lw['b_gate'].shape),
            _resident((1, width)),
        ],
        out_specs=[tok_spec, k_spec, tok_spec, k_spec] + [tok_spec] * 5 + [state_c, state_h],
        out_shape=[tok(BF16), k_shape(F32), tok(F32), k_shape(BF16), tok(BF16),
                   tok(F32), tok(F32), tok(F32), tok(F32),
                   jax.ShapeDtypeStruct((batch, CONV_WIDTH - 1, width), F32),
                   jax.ShapeDtypeStruct((batch, 1, width), F32)],
        scratch_shapes=[
            pltpu.VMEM((nb, CONV_PAD + tt, width), F32),
            pltpu.VMEM((tm, width), F32),
            pltpu.VMEM((tm, width), F32),
            pltpu.VMEM((nb, 1, width), F32),
        ],
        compiler_params=pltpu.CompilerParams(
            dimension_semantics=("arbitrary", "arbitrary"), vmem_limit_bytes=VMEM_LIMIT),
        name="mix_in",
    )(x2d, conv_prev, h_prev, lw['g_mix'], lw['w_in'], lw['w_kt'], lw['w_conv'], lw['b_conv'],
      lw['w_gate'], lw['b_gate'], lw['lru_lambda'])


def _diff_lambda(lq1_ref, lk1_ref, lq2_ref, lk2_ref, lam_init):
    s1 = jnp.sum(lq1_ref[...] * lk1_ref[...], axis=-1, keepdims=True)
    s2 = jnp.sum(lq2_ref[...] * lk2_ref[...], axis=-1, keepdims=True)
    return jnp.exp(s1) - jnp.exp(s2) + lam_init


def _head_out(o1, o2, lam, gsub, lam_init):
    od = o1 - lam * o2
    return _rms(od, gsub) * (1.0 - lam_init)


def _prompt_attn_body(i, q_ref, kt_ref, v_ref, lam, gsub_ref, o_ref,
                      qbd_scr, s_scr, p_scr, m_scr, l_scr, a_scr, acc_scr, *, tq, qk_dim, lam_init, row_chunk,
                      run_unmasked):
    tk = tq
    rows = 2 * tq
    n_hg = qbd_scr.shape[0]
    lane = lax.broadcasted_iota(jnp.int32, (tq, LANES), 1)
    for hh in range(n_hg):
        q = q_ref[:, hh * LANES:(hh + 1) * LANES]
        zero = jnp.zeros_like(q)
        qbd_scr[hh, 0:tq, :] = jnp.where(lane < qk_dim, q, zero)
        qbd_scr[hh, tq:rows, :] = jnp.where(lane >= qk_dim, q, zero)
    m_scr[...] = jnp.full(m_scr.shape, -jnp.inf, F32)
    l_scr[...] = jnp.zeros(l_scr.shape, F32)
    acc_scr[...] = jnp.zeros(acc_scr.shape, F32)

    def head_step(hh, k0, n_keys, row_blocks, key_offset=None, tick=None):
        hs = slice(hh * LANES, (hh + 1) * LANES)
        kt_blk = kt_ref[hs, pl.ds(k0, n_keys)]
        v_blk = v_ref[pl.ds(k0, n_keys), hs]
        for b0, nb in row_blocks:
            bs = slice(b0, b0 + nb)
            s_scr[hh, bs, 0:n_keys] = jnp.dot(qbd_scr[hh, bs, :], kt_blk, preferred_element_type=F32)
            for r0 in range(b0, b0 + nb, row_chunk):
                rs = slice(r0, r0 + row_chunk)
                s = s_scr[hh, rs, 0:n_keys]
                if key_offset is not None:
                    qpos = lax.broadcasted_iota(jnp.int32, (row_chunk, n_keys), 0) + (r0 % tq)
                    kpos = lax.broadcasted_iota(jnp.int32, (row_chunk, n_keys), 1) + key_offset
                    s = jnp.where(kpos <= qpos, s, -jnp.inf)
                tiles = [s[:, t * LANES:(t + 1) * LANES] for t in range(n_keys // LANES)]
                mx = functools.reduce(jnp.maximum, tiles)
                m_old = m_scr[hh, rs, :]
                m_new = jnp.maximum(m_old, jnp.max(mx, axis=-1, keepdims=True))
                alpha = jnp.exp2(m_old - m_new)
                ps = [jnp.exp2(t - m_new) for t in tiles]
                l_scr[hh, rs, :] = alpha * l_scr[hh, rs, :] + functools.reduce(jnp.add, ps)
                m_scr[hh, rs, :] = m_new
                a_scr[hh, rs, :] = alpha
                p_scr[hh, rs, 0:n_keys] = jnp.concatenate(ps, axis=1).astype(BF16)
                if tick is not None:
                    tick()
            pv = jnp.dot(p_scr[hh, bs, 0:n_keys], v_blk, preferred_element_type=F32)
            acc_scr[hh, bs, :] = a_scr[hh, bs, :] * acc_scr[hh, bs, :] + pv

    def unmasked(j, carry, fillers=()):
        k0 = j * tk if isinstance(j, int) else pl.multiple_of(j * tk, tk)
        tick = _spread(list(fillers), n_hg * (rows // row_chunk)) if fillers else None
        for hh in range(n_hg):
            head_step(hh, k0, tk, [(0, rows)], tick=tick)
        return carry

    run_unmasked(unmasked)
    d0 = pl.multiple_of(i * tk, tk)
    for hh in range(n_hg):
        head_step(hh, d0, tk, [(0, rows)], key_offset=0)
    for hh in range(n_hg):
        o = acc_scr[hh] / jnp.sum(l_scr[hh], axis=-1, keepdims=True)
        o_ref[:, hh * LANES:(hh + 1) * LANES] = _head_out(
            o[:tq], o[tq:], lam, gsub_ref[...], lam_init).astype(o_ref.dtype)


def _sample_attn_parts(q_ref, kn_ref, vn_ref, k_refs, v_refs, lam, gsub_ref, o_ref,
                       qbd_ref, kbuf, vbuf, m_ref, l_ref, acc_ref, *, n_heads, qk_dim, page, lam_init):
    tq, feat = q_ref.shape
    n_maps = 2 * n_heads
    rows = n_maps * tq
    v_dim = feat // n_heads

    def start():
        qt = jnp.concatenate([q_ref[...]] * n_maps, axis=0)
        r = lax.broadcasted_iota(jnp.int32, (rows, feat), 0)
        c = lax.broadcasted_iota(jnp.int32, (rows, feat), 1)
        qbd = jnp.where(r // tq == c // qk_dim, qt, 0.0).astype(BF16)
        qbd_ref[...] = qbd
        pad = jnp.zeros((page - tq, feat), F32)
        kn = jnp.concatenate([kn_ref[...], pad], axis=0).astype(BF16)
        vn = jnp.concatenate([vn_ref[...], pad], axis=0).astype(BF16)
        s = lax.dot_general(qbd, kn, (((1,), (1,)), ((), ())), preferred_element_type=F32)
        rr = lax.broadcasted_iota(jnp.int32, (rows, page), 0)
        cc = lax.broadcasted_iota(jnp.int32, (rows, page), 1)
        s = jnp.where(cc <= rr % tq, s, -jnp.inf)
        m = jnp.max(s, axis=-1, keepdims=True)
        p = jnp.exp2(s - m)
        m_ref[...] = m
        l_ref[...] = jnp.sum(p, axis=-1, keepdims=True)
        acc_ref[...] = jnp.dot(p.astype(BF16), vn, preferred_element_type=F32)

    def cast_page(c, g):
        src = c * PAGE_CHUNK + g
        kbuf[:, g * page:(g + 1) * page] = k_refs[src][...].astype(BF16)
        for h in range(n_heads):
            vh = v_refs[src][pl.ds(h, page, stride=n_heads), :]
            vbuf[g * page:(g + 1) * page, h * v_dim:(h + 1) * v_dim] = vh.astype(BF16)

    def reduce_chunk():
        s = jnp.dot(qbd_ref[...], kbuf[...], preferred_element_type=F32)
        m = m_ref[...]
        m_new = jnp.maximum(m, jnp.max(s, axis=-1, keepdims=True))
        alpha = jnp.exp2(m - m_new)
        p = jnp.exp2(s - m_new)
        l_ref[...] = alpha * l_ref[...] + jnp.sum(p, axis=-1, keepdims=True)
        acc_ref[...] = alpha * acc_ref[...] + jnp.dot(p.astype(BF16), vbuf[...], preferred_element_type=F32)
        m_ref[...] = m_new

    def finish():
        o = acc_ref[...] / l_ref[...]
        for h in range(n_heads):
            cols = slice(h * v_dim, (h + 1) * v_dim)
            o1 = o[(2 * h) * tq:(2 * h + 1) * tq, cols]
            o2 = o[(2 * h + 1) * tq:(2 * h + 2) * tq, cols]
            o_ref[:, cols] = _head_out(o1, o2, lam, gsub_ref[...], lam_init)

    return start, cast_page, reduce_chunk, finish


def _spread(fillers, n_slots):
    state = [0, 0]

    def tick():
        state[0] += 1
        target = -(-len(fillers) * state[0] // n_slots)
        while state[1] < target:
            fillers[state[1]]()
            state[1] += 1

    return tick


def _page_copies(pt_ref, kt_hbm, v_hbm, kpg, vpg, sem, step, slot, *, n_pg, steps_per_sample):
    batch_idx = step // steps_per_sample
    first = (step % steps_per_sample) * n_pg
    copies = []
    for g in range(n_pg):
        pid = pt_ref[batch_idx, first + g]
        copies.append(pltpu.make_async_copy(kt_hbm.at[pid], kpg.at[slot, g], sem.at[slot]))
        copies.append(pltpu.make_async_copy(v_hbm.at[pid], vpg.at[slot, g], sem.at[slot]))
    return copies


def _attn_kernel(pt_ref, q_ref, kt_ref, v_ref, qs_ref, kn_ref, vn_ref, kt_hbm, v_hbm,
                 lq1_ref, lk1_ref, lq2_ref, lk2_ref, gsub_ref, o_ref, os_ref,
                 qbd_scr, s_scr, p_scr, m_scr, l_scr, a_scr, acc_scr,
                 sqbd_scr, kbuf, vbuf, sm_scr, sl_scr, sacc_scr, kpg, vpg, page_sem,
                 *, n_pg, steps_per_sample, tq, n_heads, qk_dim, page, lam_init):
    i = pl.program_id(2)
    step = (pl.program_id(0) * pl.num_programs(1) + pl.program_id(1)) * pl.num_programs(2) + i
    total_steps = pl.num_programs(0) * pl.num_programs(1) * pl.num_programs(2)
    slot = step % 2
    copies = functools.partial(_page_copies, pt_ref, kt_hbm, v_hbm, kpg, vpg, page_sem,
                               n_pg=n_pg, steps_per_sample=steps_per_sample)

    @pl.when(step == 0)
    def _():
        for c in copies(step, slot):
            c.start()

    @pl.when(step + 1 < total_steps)
    def _():
        for c in copies(step + 1, 1 - slot):
            c.start()

    for c in copies(step, slot):
        c.wait()
    k_refs = [kpg.at[slot, g] for g in range(n_pg)]
    v_refs = [vpg.at[slot, g] for g in range(n_pg)]
    lam = _diff_lambda(lq1_ref, lk1_ref, lq2_ref, lk2_ref, lam_init)
    sample_start, cast_page, reduce_chunk, sample_finish = _sample_attn_parts(
        qs_ref, kn_ref, vn_ref, k_refs, v_refs, lam, gsub_ref, os_ref,
        sqbd_scr, kbuf, vbuf, sm_scr, sl_scr, sacc_scr,
        n_heads=n_heads, qk_dim=qk_dim, page=page, lam_init=lam_init)
    n_chunks = n_pg // PAGE_CHUNK
    sub_step = step % steps_per_sample
    pl.when(sub_step == 0)(sample_start)

    def run_unmasked(key_block):
        casts = lambda c: [functools.partial(cast_page, c, g) for g in range(PAGE_CHUNK)]
        for n_fat in range(n_chunks + 1):
            def branch(n_fat=n_fat):
                for c in range(n_fat):
                    key_block(c, 0, casts(c))
                    reduce_chunk()
                for c in range(n_fat, n_chunks):
                    for cast in casts(c):
                        cast()
                    reduce_chunk()
                if n_fat == n_chunks:
                    lax.fori_loop(n_chunks, i, key_block, 0)
            pl.when(i >= n_fat if n_fat == n_chunks else i == n_fat)(branch)

    _prompt_attn_body(i, q_ref, kt_ref, v_ref, lam, gsub_ref, o_ref,
                      qbd_scr, s_scr, p_scr, m_scr, l_scr, a_scr, acc_scr,
                      tq=tq, qk_dim=qk_dim, lam_init=lam_init, row_chunk=32, run_unmasked=run_unmasked)
    pl.when(sub_step == steps_per_sample - 1)(sample_finish)


def _attention(q, kt, v, qs, k_new, v_new, kt_pages, v_pages, page_table, lw, *,
               n_heads, qk_dim, lam_init, tq, n_hg):
    batch, seq, feat = q.shape
    dec_batch, dec_seq, _ = qs.shape
    page = kt_pages.shape[-1]
    n_pages = page_table.shape[1]
    v_dim = feat // n_heads
    n_groups = n_heads // n_hg
    n_q = seq // tq
    total_steps = batch * n_groups * n_q
    assert 2 * qk_dim == LANES and v_dim == LANES and page == LANES and n_heads % n_hg == 0
    assert (dec_batch * n_pages) % total_steps == 0
    n_pg = dec_batch * n_pages // total_steps
    assert n_pages % n_pg == 0 and n_pg % PAGE_CHUNK == 0
    steps_per_sample = n_pages // n_pg
    rows = 2 * tq
    srows = 2 * n_heads * dec_seq
    gw = n_hg * LANES

    def linear(b, h, i):
        return (b * n_groups + h) * n_q + i

    small = lambda n: pl.BlockSpec((1, n), lambda b, h, i, pt: (0, 0))
    new_spec = pl.BlockSpec((None, dec_seq, feat), lambda b, h, i, pt: (linear(b, h, i) // steps_per_sample, 0, 0))

    kern = functools.partial(_attn_kernel, n_pg=n_pg, steps_per_sample=steps_per_sample, tq=tq,
                             n_heads=n_heads, qk_dim=qk_dim, page=page, lam_init=lam_init)
    grid_spec = pltpu.PrefetchScalarGridSpec(
        num_scalar_prefetch=1,
        grid=(batch, n_groups, n_q),
        in_specs=[
            pl.BlockSpec((None, tq, gw), lambda b, h, i, pt: (b, i, h)),
            pl.BlockSpec((None, gw, seq), lambda b, h, i, pt: (b, h, 0), pipeline_mode=pl.Buffered(1)),
            pl.BlockSpec((None, seq, gw), lambda b, h, i, pt: (b, 0, h), pipeline_mode=pl.Buffered(1)),
            new_spec, new_spec, new_spec,
            pl.BlockSpec(memory_space=pl.ANY), pl.BlockSpec(memory_space=pl.ANY)]
        + [small(qk_dim)] * 4 + [small(v_dim)],
        out_specs=[pl.BlockSpec((None, tq, gw), lambda b, h, i, pt: (b, i, h)), new_spec],
        scratch_shapes=[
            pltpu.VMEM((n_hg, rows, 2 * qk_dim), BF16),
            pltpu.VMEM((n_hg, rows, tq), F32),
            pltpu.VMEM((n_hg, rows, tq), BF16),
            pltpu.VMEM((n_hg, rows, LANES), F32),
            pltpu.VMEM((n_hg, rows, LANES), F32),
            pltpu.VMEM((n_hg, rows, LANES), F32),
            pltpu.VMEM((n_hg, rows, v_dim), F32),
            pltpu.VMEM((srows, feat), BF16),
            pltpu.VMEM((feat, PAGE_CHUNK * page), BF16),
            pltpu.VMEM((PAGE_CHUNK * page, feat), BF16),
            pltpu.VMEM((srows, 1), F32),
            pltpu.VMEM((srows, 1), F32),
            pltpu.VMEM((srows, feat), F32),
            pltpu.VMEM((2, n_pg, feat, page), F32),
            pltpu.VMEM((2, n_pg, page * n_heads, v_dim), F32),
            pltpu.SemaphoreType.DMA((2,)),
        ],
    )
    return pl.pallas_call(
        kern,
        grid_spec=grid_spec,
        out_shape=[jax.ShapeDtypeStruct((batch, seq, feat), BF16),
                   jax.ShapeDtypeStruct((dec_batch, dec_seq, feat), F32)],
        compiler_params=pltpu.CompilerParams(
            dimension_semantics=("arbitrary", "arbitrary", "arbitrary"), vmem_limit_bytes=VMEM_LIMIT),
        name="attention",
    )(page_table, q, kt, v, qs, k_new, v_new, kt_pages, v_pages,
      lw['lq1'], lw['lk1'], lw['lq2'], lw['lk2'], lw['g_subln'])


def _mix_out_kernel(x_ref, oa_ref, hs_ref, yr_ref, ga_ref, gr_ref, p_ref,
                    wattn_ref, wrec_ref, wout_ref, gffn_ref, wg_ref, wu_ref, wd_ref,
                    gple_ref, wpg_ref, wpp_ref, gfin_ref, y_ref, *, ff_chunk, final_norm):
    dot = functools.partial(jnp.dot, preferred_element_type=F32)
    o_a = dot(oa_ref[...].astype(BF16), wattn_ref[...])
    o_r = dot((hs_ref[...] * jax.nn.gelu(yr_ref[...])).astype(BF16), wrec_ref[...])
    m = jax.nn.sigmoid(ga_ref[...]) * o_a + jax.nn.sigmoid(gr_ref[...]) * o_r
    h = x_ref[...] + dot(m.astype(BF16), wout_ref[...])

    u2 = _rms(h, gffn_ref[...]).astype(BF16)
    d_ff = wg_ref.shape[1]
    ffn = jnp.zeros_like(h)
    for c0 in range(0, d_ff, ff_chunk):
        c1 = min(c0 + ff_chunk, d_ff)
        hid = jax.nn.silu(dot(u2, wg_ref[:, c0:c1])) * dot(u2, wu_ref[:, c0:c1])
        ffn = ffn + dot(hid.astype(BF16), wd_ref[c0:c1, :])
    h = h + ffn

    g = jax.nn.sigmoid(dot(_rms(h, gple_ref[...]).astype(BF16), wpg_ref[...]))
    h = h + g * dot(p_ref[...].astype(BF16), wpp_ref[...])
    y_ref[...] = _rms(h, gfin_ref[...]) if final_norm else h


def _mix_out(x2d, oa, hs, yr, ga, gr, p2d, lw, g_final, *, tm, final_norm):
    n_tok, d_model = x2d.shape
    tok_spec = lambda w: pl.BlockSpec((tm, w), lambda i: (i, 0))
    weights = [lw['w_attn_br'], lw['w_rec_br'], lw['w_out'], lw['g_ffn'], lw['w_ffn_gate'], lw['w_ffn_up'],
               lw['w_ffn_down'], lw['g_ple'], lw['w_ple_gate'], lw['w_ple_proj'], g_final]
    kern = functools.partial(_mix_out_kernel, ff_chunk=1024, final_norm=final_norm)
    return pl.pallas_call(
        kern,
        grid=(n_tok // tm,),
        in_specs=[tok_spec(d_model), tok_spec(oa.shape[1]), tok_spec(hs.shape[1]), tok_spec(yr.shape[1]),
                  tok_spec(d_model), tok_spec(d_model), tok_spec(p2d.shape[1])]
        + [_resident(w.shape) for w in weights],
        out_specs=tok_spec(d_model),
        out_shape=jax.ShapeDtypeStruct((n_tok, d_model), F32),
        compiler_params=pltpu.CompilerParams(
            dimension_semantics=("arbitrary",), vmem_limit_bytes=VMEM_LIMIT),
        name="mix_out",
    )(x2d, oa, hs, yr, ga, gr, p2d, *weights)


def kernel(x_prompt, x_sample, p_prompt, p_sample, cache_k, cache_v, page_table, state_conv, state_h, g_mix, w_in, lambda_q1, lambda_k1, lambda_q2, lambda_k2, g_subln, w_attn_br, w_conv, b_conv, w_gate_a, b_gate_a, w_gate_x, b_gate_x, lru_lambda, w_rec_br, w_out, g_ffn, w_ffn_gate, w_ffn_up, w_ffn_down, g_ple, w_ple_gate, w_ple_proj, g_final):
    depth = w_in.shape[0]
    batch, seq, d_model = x_prompt.shape
    dec_batch, dec_seq, _ = x_sample.shape
    _, n_phys, page, n_heads, _, qk_dim = cache_k.shape
    v_dim = cache_v.shape[-1]
    width = w_conv.shape[-1]
    feat = n_heads * v_dim
    qk_scale = qk_dim ** -0.5 * math.log2(math.e)

    hp = x_prompt.reshape(batch * seq, d_model)
    hs = x_sample.reshape(dec_batch * dec_seq, d_model)
    outs = [[] for _ in range(8)]
    for l in range(depth):
        lam_init = 0.8 - 0.6 * math.exp(-0.3 * l)
        last = l == depth - 1
        lw = dict(
            g_mix=g_mix[l][None], w_in=w_in[l].astype(BF16),
            w_kt=w_in[l][:, feat:2 * feat].T.astype(BF16),
            lq1=lambda_q1[l][None], lk1=lambda_k1[l][None], lq2=lambda_q2[l][None], lk2=lambda_k2[l][None],
            g_subln=g_subln[l][None], w_attn_br=w_attn_br[l].astype(BF16),
            w_conv=w_conv[l], b_conv=b_conv[l][None],
            w_gate=jnp.concatenate([w_gate_a[l], w_gate_x[l]], axis=-1).astype(BF16),
            b_gate=jnp.concatenate([b_gate_a[l], b_gate_x[l]], axis=-1)[:, None, :],
            lru_lambda=lru_lambda[l][None], w_rec_br=w_rec_br[l].astype(BF16), w_out=w_out[l].astype(BF16),
            g_ffn=g_ffn[l][None], w_ffn_gate=w_ffn_gate[l].astype(BF16), w_ffn_up=w_ffn_up[l].astype(BF16),
            w_ffn_down=w_ffn_down[l].astype(BF16), g_ple=g_ple[l][None],
            w_ple_gate=w_ple_gate[l].astype(BF16), w_ple_proj=w_ple_proj[l].astype(BF16))
        gfin = g_final[None]

        conv0 = jnp.zeros((batch, CONV_WIDTH - 1, width), F32)
        h0 = jnp.zeros((batch, 1, width), F32)
        q, kf, vf, kb, vb, rec, yr, ga, gr, cnew, hnew = _mix_in(
            hp, conv0, h0, lw, batch=batch, seq=seq, nb=1, tt=256, qk_scale=qk_scale, k_transposed=True)
        qs, kfs, vfs, _, _, recs, yrs, gas, grs, cnews, hnews = _mix_in(
            hs, state_conv[l], state_h[l][:, None, :], lw,
            batch=dec_batch, seq=dec_seq, nb=dec_batch, tt=dec_seq, qk_scale=qk_scale, k_transposed=False)

        kt_pages = jnp.transpose(cache_k[l], (0, 2, 3, 4, 1)).reshape(n_phys, feat, page)
        v_pages = cache_v[l].reshape(n_phys, page * n_heads, v_dim)
        oa, oas = _attention(
            q.reshape(batch, seq, feat), kb, vb.reshape(batch, seq, feat),
            qs.astype(F32).reshape(dec_batch, dec_seq, feat),
            kfs.reshape(dec_batch, dec_seq, feat), vfs.reshape(dec_batch, dec_seq, feat),
            kt_pages, v_pages, page_table, lw,
            n_heads=n_heads, qk_dim=qk_dim, lam_init=lam_init, tq=512, n_hg=2)

        hp = _mix_out(hp, oa.reshape(batch * seq, feat), rec, yr, ga, gr,
                      p_prompt[l].reshape(batch * seq, -1), lw, gfin, tm=256, final_norm=last)
        hs = _mix_out(hs, oas.reshape(dec_batch * dec_seq, feat), recs, yrs, gas, grs,
                      p_sample[l].reshape(dec_batch * dec_seq, -1), lw, gfin,
                      tm=dec_batch * dec_seq, final_norm=last)

        outs[0].append(jnp.transpose(kf.reshape(batch, n_heads, 2, qk_dim, seq), (0, 4, 1, 2, 3)))
        outs[1].append(vf.reshape(batch, seq, n_heads, v_dim))
        outs[2].append(cnew)
        outs[3].append(hnew.reshape(batch, width))
        outs[4].append(kfs.reshape(dec_batch, dec_seq, n_heads, 2, qk_dim))
        outs[5].append(vfs.reshape(dec_batch, dec_seq, n_heads, v_dim))
        outs[6].append(cnews)
        outs[7].append(hnews.reshape(dec_batch, width))

    y_prompt = hp.reshape(batch, seq, d_model)
    y_sample = hs.reshape(dec_batch, dec_seq, d_model)
    return (y_prompt, y_sample) + tuple(jnp.stack(o) for o in outs)
```

```python
import functools
import math

import jax
import jax.numpy as jnp
from jax import lax
from jax.experimental import pallas as pl
from jax.experimental.pallas import tpu as pltpu

F32 = jnp.float32
BF16 = jnp.bfloat16

EPS = 1e-6
LRU_C = 8.0
CONV_WIDTH = 4
N_LRU_BLOCKS = 8
SUBLANES = 8
LANES = 128
CONV_PAD = 8
PAGE_CHUNK = 8
FILL_COLS = 512
VMEM_LIMIT = 60 * 1024 * 1024


def _rms(x, g):
    ms = jnp.mean(x * x, axis=-1, keepdims=True)
    return x * lax.rsqrt(ms + EPS) * g


def _softplus(x):
    return jnp.maximum(x, 0.0) + jnp.log1p(jnp.exp(-jnp.abs(x)))


def _resident(shape):
    nd = len(shape)
    return pl.BlockSpec(shape, lambda *_: (0,) * nd, pipeline_mode=pl.Buffered(1))


def _mix_in_kernel(x_ref, cprev_ref, hprev_ref, gmix_ref, win_ref, wconv_ref, bconv_ref,
                   wgate_ref, bgate_ref, lam_ref,
                   q_ref, kf_ref, vf_ref, kb_ref, vb_ref, hs_ref, yr_ref, ga_ref, gr_ref,
                   cnew_ref, hnew_ref,
                   xbuf, abuf, ubuf, hstate, *, nb, tt, width, qk_scale, k_transposed):
    t = pl.program_id(1)
    tm = nb * tt
    lru_block = width // N_LRU_BLOCKS

    @pl.when(t == 0)
    def _():
        xbuf[:, CONV_PAD - (CONV_WIDTH - 1):CONV_PAD, :] = cprev_ref[...]
        hstate[...] = hprev_ref[...]

    u = _rms(x_ref[...], gmix_ref[...]).astype(BF16)

    def proj(i, c0=0, c1=width):
        return jnp.dot(u, win_ref[:, i * width + c0:i * width + c1], preferred_element_type=F32)

    def emit_q(c0, c1):
        q_ref[:, c0:c1] = (proj(0, c0, c1) * qk_scale).astype(BF16)

    def emit_k(c0, c1):
        if k_transposed:
            k = lax.dot_general(win_ref[:, width + c0:width + c1], u, (((0,), (1,)), ((), ())),
                                preferred_element_type=F32)
            kf_ref[c0:c1, :] = k
            kb_ref[c0:c1, :] = k.astype(BF16)
        else:
            k = proj(1, c0, c1)
            kf_ref[:, c0:c1] = k
            kb_ref[:, c0:c1] = k.astype(BF16)

    def emit_v(c0, c1):
        v = proj(2, c0, c1)
        vf_ref[:, c0:c1] = v
        vb_ref[:, c0:c1] = v.astype(BF16)

    def emit_raw(i, ref, c0, c1):
        ref[:, c0:c1] = proj(i, c0, c1)

    fillers = []
    for c0 in range(0, width, FILL_COLS):
        c1 = c0 + FILL_COLS
        fillers += [functools.partial(emit_q, c0, c1), functools.partial(emit_k, c0, c1),
                    functools.partial(emit_v, c0, c1), functools.partial(emit_raw, 4, yr_ref, c0, c1),
                    functools.partial(emit_raw, 5, ga_ref, c0, c1), functools.partial(emit_raw, 6, gr_ref, c0, c1)]
    w_conv, w_gate, w_scan = 4, 6, 1
    total_weight = w_conv + N_LRU_BLOCKS * w_gate + nb * (tt // SUBLANES) * w_scan
    done = [0, 0]

    def fill(weight):
        done[0] += weight
        target = -(-len(fillers) * done[0] // total_weight)
        while done[1] < target:
            fillers[done[1]]()
            done[1] += 1

    xbuf[:, CONV_PAD:CONV_PAD + tt, :] = proj(3).reshape(nb, tt, width)
    fill(w_conv)
    xc = bconv_ref[...].reshape(1, 1, width)
    for j in range(CONV_WIDTH):
        r0 = CONV_PAD - (CONV_WIDTH - 1) + j
        xc = xc + xbuf[:, r0:r0 + tt, :] * wconv_ref[j:j + 1, :].reshape(1, 1, width)
    tail = xbuf[:, CONV_PAD + tt - (CONV_WIDTH - 1):CONV_PAD + tt, :]
    cnew_ref[...] = tail
    xbuf[:, CONV_PAD - (CONV_WIDTH - 1):CONV_PAD, :] = tail

    xc = xc.reshape(tm, width)
    xcb = xc.astype(BF16)
    sp = _softplus(-lam_ref[...])
    for blk in range(N_LRU_BLOCKS):
        fill(w_gate)
        sl = slice(blk * lru_block, (blk + 1) * lru_block)
        g = jnp.dot(xcb[:, sl], wgate_ref[blk], preferred_element_type=F32) + bgate_ref[blk]
        r = jax.nn.sigmoid(g[:, :lru_block])
        ig = jax.nn.sigmoid(g[:, lru_block:])
        a = jnp.exp(-LRU_C * r * sp[:, sl])
        abuf[:, sl] = a
        ubuf[:, sl] = jnp.sqrt(1.0 - a * a) * (ig * xc[:, sl])

    n_groups = tt // SUBLANES
    row = lax.broadcasted_iota(jnp.int32, (SUBLANES, width), 0)

    for b in range(nb):
        h = hstate[b]
        for g in range(n_groups):
            fill(w_scan)
            r0 = (b * n_groups + g) * SUBLANES
            a = abuf[r0:r0 + SUBLANES, :]
            uu = ubuf[r0:r0 + SUBLANES, :]
            d = 1
            while d < SUBLANES:
                a_sh = jnp.where(row >= d, pltpu.roll(a, d, 0), 1.0)
                u_sh = jnp.where(row >= d, pltpu.roll(uu, d, 0), 0.0)
                uu = a * u_sh + uu
                a = a * a_sh
                d *= 2
            hb = uu + a * h
            hs_ref[r0:r0 + SUBLANES, :] = hb
            h = hb[SUBLANES - 1:SUBLANES, :]
        hstate[b] = h
    hnew_ref[...] = hstate[...]
    assert done[1] == len(fillers)


def _mix_in(x2d, conv_prev, h_prev, lw, *, batch, seq, nb, tt, qk_scale, k_transposed):
    n_tok, d_model = x2d.shape
    width = lw['w_conv'].shape[-1]
    n_t = seq // tt
    tm = nb * tt
    grid = (batch // nb, n_t)
    tok = lambda dt: jax.ShapeDtypeStruct((n_tok, width), dt)
    tok_spec = pl.BlockSpec((tm, width), lambda b, t: (b * n_t + t, 0))
    if k_transposed:
        assert nb == 1
        k_shape = lambda dt: jax.ShapeDtypeStruct((batch, width, seq), dt)
        k_spec = pl.BlockSpec((None, width, tt), lambda b, t: (b, 0, t))
    else:
        k_shape, k_spec = tok, tok_spec
    state_c = pl.BlockSpec((nb, CONV_WIDTH - 1, width), lambda b, t: (b, 0, 0))
    state_h = pl.BlockSpec((nb, 1, width), lambda b, t: (b, 0, 0))
    kern = functools.partial(_mix_in_kernel, nb=nb, tt=tt, width=width, qk_scale=qk_scale,
                             k_transposed=k_transposed)
    return pl.pallas_call(
        kern,
        grid=grid,
        in_specs=[
            pl.BlockSpec((tm, d_model), lambda b, t: (b * n_t + t, 0)),
            state_c, state_h,
            _resident((1, d_model)),
            _resident(lw['w_in'].shape),
            _resident((CONV_WIDTH, width)),
            _resident((1, width)),
            _resident(lw['w_gate'].shape),
            _resident(lw['b_gate'].shape),
            _resident((1, width)),
        ],
        out_specs=[tok_spec, k_spec, tok_spec, k_spec] + [tok_spec] * 5 + [state_c, state_h],
        out_shape=[tok(BF16), k_shape(F32), tok(F32), k_shape(BF16), tok(BF16),
                   tok(F32), tok(F32), tok(F32), tok(F32),
                   jax.ShapeDtypeStruct((batch, CONV_WIDTH - 1, width), F32),
                   jax.ShapeDtypeStruct((batch, 1, width), F32)],
        scratch_shapes=[
            pltpu.VMEM((nb, CONV_PAD + tt, width), F32),
            pltpu.VMEM((tm, width), F32),
            pltpu.VMEM((tm, width), F32),
            pltpu.VMEM((nb, 1, width), F32),
        ],
        compiler_params=pltpu.CompilerParams(
            dimension_semantics=("arbitrary", "arbitrary"), vmem_limit_bytes=VMEM_LIMIT),
        name="mix_in",
    )(x2d, conv_prev, h_prev, lw['g_mix'], lw['w_in'], lw['w_conv'], lw['b_conv'],
      lw['w_gate'], lw['b_gate'], lw['lru_lambda'])


def _diff_lambda(lq1_ref, lk1_ref, lq2_ref, lk2_ref, lam_init):
    s1 = jnp.sum(lq1_ref[...] * lk1_ref[...], axis=-1, keepdims=True)
    s2 = jnp.sum(lq2_ref[...] * lk2_ref[...], axis=-1, keepdims=True)
    return jnp.exp(s1) - jnp.exp(s2) + lam_init


def _head_out(o1, o2, lam, gsub, lam_init):
    od = o1 - lam * o2
    return _rms(od, gsub) * (1.0 - lam_init)


def _prompt_attn_body(i, q_ref, kt_ref, v_ref, lam, gsub_ref, o_ref,
                      qbd_scr, s_scr, p_scr, m_scr, l_scr, a_scr, acc_scr, *, tq, qk_dim, lam_init, row_chunk,
                      run_unmasked):
    tk = tq
    rows = 2 * tq
    n_hg = qbd_scr.shape[0]
    lane = lax.broadcasted_iota(jnp.int32, (tq, LANES), 1)
    for hh in range(n_hg):
        q = q_ref[:, hh * LANES:(hh + 1) * LANES]
        zero = jnp.zeros_like(q)
        qbd_scr[hh, 0:tq, :] = jnp.where(lane < qk_dim, q, zero)
        qbd_scr[hh, tq:rows, :] = jnp.where(lane >= qk_dim, q, zero)
    m_scr[...] = jnp.full(m_scr.shape, -jnp.inf, F32)
    l_scr[...] = jnp.zeros(l_scr.shape, F32)
    acc_scr[...] = jnp.zeros(acc_scr.shape, F32)

    def head_step(hh, k0, n_keys, row_blocks, key_offset=None, tick=None):
        hs = slice(hh * LANES, (hh + 1) * LANES)
        kt_blk = kt_ref[hs, pl.ds(k0, n_keys)]
        v_blk = v_ref[pl.ds(k0, n_keys), hs]
        for b0, nb in row_blocks:
            bs = slice(b0, b0 + nb)
            s_scr[hh, bs, 0:n_keys] = jnp.dot(qbd_scr[hh, bs, :], kt_blk, preferred_element_type=F32)
            for r0 in range(b0, b0 + nb, row_chunk):
                rs = slice(r0, r0 + row_chunk)
                s = s_scr[hh, rs, 0:n_keys]
                if key_offset is not None:
                    qpos = lax.broadcasted_iota(jnp.int32, (row_chunk, n_keys), 0) + (r0 % tq)
                    kpos = lax.broadcasted_iota(jnp.int32, (row_chunk, n_keys), 1) + key_offset
                    s = jnp.where(kpos <= qpos, s, -jnp.inf)
                tiles = [s[:, t * LANES:(t + 1) * LANES] for t in range(n_keys // LANES)]
                mx = functools.reduce(jnp.maximum, tiles)
                m_old = m_scr[hh, rs, :]
                m_new = jnp.maximum(m_old, jnp.max(mx, axis=-1, keepdims=True))
                alpha = jnp.exp2(m_old - m_new)
                ps = [jnp.exp2(t - m_new) for t in tiles]
                l_scr[hh, rs, :] = alpha * l_scr[hh, rs, :] + functools.reduce(jnp.add, ps)
                m_scr[hh, rs, :] = m_new
                a_scr[hh, rs, :] = alpha
                p_scr[hh, rs, 0:n_keys] = jnp.concatenate(ps, axis=1).astype(BF16)
                if tick is not None:
                    tick()
            pv = jnp.dot(p_scr[hh, bs, 0:n_keys], v_blk, preferred_element_type=F32)
            acc_scr[hh, bs, :] = a_scr[hh, bs, :] * acc_scr[hh, bs, :] + pv

    def unmasked(j, carry, fillers=()):
        k0 = j * tk if isinstance(j, int) else pl.multiple_of(j * tk, tk)
        tick = _spread(list(fillers), n_hg * (rows // row_chunk)) if fillers else None
        for hh in range(n_hg):
            head_step(hh, k0, tk, [(0, rows)], tick=tick)
        return carry

    run_unmasked(unmasked)
    d0 = pl.multiple_of(i * tk, tk)
    for hh in range(n_hg):
        head_step(hh, d0, tk, [(0, rows)], key_offset=0)
    for hh in range(n_hg):
        o = acc_scr[hh] / jnp.sum(l_scr[hh], axis=-1, keepdims=True)
        o_ref[:, hh * LANES:(hh + 1) * LANES] = _head_out(
            o[:tq], o[tq:], lam, gsub_ref[...], lam_init).astype(o_ref.dtype)


def _sample_attn_parts(q_ref, kn_ref, vn_ref, k_refs, v_refs, lam, gsub_ref, o_ref,
                       qbd_ref, kbuf, vbuf, m_ref, l_ref, acc_ref, *, n_heads, qk_dim, page, lam_init):
    tq, feat = q_ref.shape
    n_maps = 2 * n_heads
    rows = n_maps * tq
    v_dim = feat // n_heads

    def start():
        qt = jnp.concatenate([q_ref[...]] * n_maps, axis=0)
        r = lax.broadcasted_iota(jnp.int32, (rows, feat), 0)
        c = lax.broadcasted_iota(jnp.int32, (rows, feat), 1)
        qbd = jnp.where(r // tq == c // qk_dim, qt, 0.0).astype(BF16)
        qbd_ref[...] = qbd
        pad = jnp.zeros((page - tq, feat), F32)
        kn = jnp.concatenate([kn_ref[...], pad], axis=0).astype(BF16)
        vn = jnp.concatenate([vn_ref[...], pad], axis=0).astype(BF16)
        s = lax.dot_general(qbd, kn, (((1,), (1,)), ((), ())), preferred_element_type=F32)
        rr = lax.broadcasted_iota(jnp.int32, (rows, page), 0)
        cc = lax.broadcasted_iota(jnp.int32, (rows, page), 1)
        s = jnp.where(cc <= rr % tq, s, -jnp.inf)
        m = jnp.max(s, axis=-1, keepdims=True)
        p = jnp.exp2(s - m)
        m_ref[...] = m
        l_ref[...] = jnp.sum(p, axis=-1, keepdims=True)
        acc_ref[...] = jnp.dot(p.astype(BF16), vn, preferred_element_type=F32)

    def cast_page(c, g):
        src = c * PAGE_CHUNK + g
        kbuf[:, g * page:(g + 1) * page] = k_refs[src][...].astype(BF16)
        for h in range(n_heads):
            vh = v_refs[src][pl.ds(h, page, stride=n_heads), :]
            vbuf[g * page:(g + 1) * page, h * v_dim:(h + 1) * v_dim] = vh.astype(BF16)

    def reduce_chunk():
        s = jnp.dot(qbd_ref[...], kbuf[...], preferred_element_type=F32)
        m = m_ref[...]
        m_new = jnp.maximum(m, jnp.max(s, axis=-1, keepdims=True))
        alpha = jnp.exp2(m - m_new)
        p = jnp.exp2(s - m_new)
        l_ref[...] = alpha * l_ref[...] + jnp.sum(p, axis=-1, keepdims=True)
        acc_ref[...] = alpha * acc_ref[...] + jnp.dot(p.astype(BF16), vbuf[...], preferred_element_type=F32)
        m_ref[...] = m_new

    def finish():
        o = acc_ref[...] / l_ref[...]
        for h in range(n_heads):
            cols = slice(h * v_dim, (h + 1) * v_dim)
            o1 = o[(2 * h) * tq:(2 * h + 1) * tq, cols]
            o2 = o[(2 * h + 1) * tq:(2 * h + 2) * tq, cols]
            o_ref[:, cols] = _head_out(o1, o2, lam, gsub_ref[...], lam_init)

    return start, cast_page, reduce_chunk, finish


def _spread(fillers, n_slots):
    state = [0, 0]

    def tick():
        state[0] += 1
        target = -(-len(fillers) * state[0] // n_slots)
        while state[1] < target:
            fillers[state[1]]()
            state[1] += 1

    return tick


def _page_copies(pt_ref, kt_hbm, v_hbm, kpg, vpg, sem, step, slot, *, n_pg, steps_per_sample):
    batch_idx = step // steps_per_sample
    first = (step % steps_per_sample) * n_pg
    copies = []
    for g in range(n_pg):
        pid = pt_ref[batch_idx, first + g]
        copies.append(pltpu.make_async_copy(kt_hbm.at[pid], kpg.at[slot, g], sem.at[slot]))
        copies.append(pltpu.make_async_copy(v_hbm.at[pid], vpg.at[slot, g], sem.at[slot]))
    return copies


def _attn_kernel(pt_ref, q_ref, kt_ref, v_ref, qs_ref, kn_ref, vn_ref, kt_hbm, v_hbm,
                 lq1_ref, lk1_ref, lq2_ref, lk2_ref, gsub_ref, o_ref, os_ref,
                 qbd_scr, s_scr, p_scr, m_scr, l_scr, a_scr, acc_scr,
                 sqbd_scr, kbuf, vbuf, sm_scr, sl_scr, sacc_scr, kpg, vpg, page_sem,
                 *, n_pg, steps_per_sample, tq, n_heads, qk_dim, page, lam_init):
    i = pl.program_id(2)
    step = (pl.program_id(0) * pl.num_programs(1) + pl.program_id(1)) * pl.num_programs(2) + i
    total_steps = pl.num_programs(0) * pl.num_programs(1) * pl.num_programs(2)
    slot = step % 2
    copies = functools.partial(_page_copies, pt_ref, kt_hbm, v_hbm, kpg, vpg, page_sem,
                               n_pg=n_pg, steps_per_sample=steps_per_sample)

    @pl.when(step == 0)
    def _():
        for c in copies(step, slot):
            c.start()

    @pl.when(step + 1 < total_steps)
    def _():
        for c in copies(step + 1, 1 - slot):
            c.start()

    for c in copies(step, slot):
        c.wait()
    k_refs = [kpg.at[slot, g] for g in range(n_pg)]
    v_refs = [vpg.at[slot, g] for g in range(n_pg)]
    lam = _diff_lambda(lq1_ref, lk1_ref, lq2_ref, lk2_ref, lam_init)
    sample_start, cast_page, reduce_chunk, sample_finish = _sample_attn_parts(
        qs_ref, kn_ref, vn_ref, k_refs, v_refs, lam, gsub_ref, os_ref,
        sqbd_scr, kbuf, vbuf, sm_scr, sl_scr, sacc_scr,
        n_heads=n_heads, qk_dim=qk_dim, page=page, lam_init=lam_init)
    n_chunks = n_pg // PAGE_CHUNK
    sub_step = step % steps_per_sample
    pl.when(sub_step == 0)(sample_start)

    def run_unmasked(key_block):
        casts = lambda c: [functools.partial(cast_page, c, g) for g in range(PAGE_CHUNK)]
        for n_fat in range(n_chunks + 1):
            def branch(n_fat=n_fat):
                for c in range(n_fat):
                    key_block(c, 0, casts(c))
                    reduce_chunk()
                for c in range(n_fat, n_chunks):
                    for cast in casts(c):
                        cast()
                    reduce_chunk()
                if n_fat == n_chunks:
                    lax.fori_loop(n_chunks, i, key_block, 0)
            pl.when(i >= n_fat if n_fat == n_chunks else i == n_fat)(branch)

    _prompt_attn_body(i, q_ref, kt_ref, v_ref, lam, gsub_ref, o_ref,
                      qbd_scr, s_scr, p_scr, m_scr, l_scr, a_scr, acc_scr,
                      tq=tq, qk_dim=qk_dim, lam_init=lam_init, row_chunk=32, run_unmasked=run_unmasked)
    pl.when(sub_step == steps_per_sample - 1)(sample_finish)


def _attention(q, kt, v, qs, k_new, v_new, kt_pages, v_pages, page_table, lw, *,
               n_heads, qk_dim, lam_init, tq, n_hg):
    batch, seq, feat = q.shape
    dec_batch, dec_seq, _ = qs.shape
    page = kt_pages.shape[-1]
    n_pages = page_table.shape[1]
    v_dim = feat // n_heads
    n_groups = n_heads // n_hg
    n_q = seq // tq
    total_steps = batch * n_groups * n_q
    assert 2 * qk_dim == LANES and v_dim == LANES and page == LANES and n_heads % n_hg == 0
    assert (dec_batch * n_pages) % total_steps == 0
    n_pg = dec_batch * n_pages // total_steps
    assert n_pages % n_pg == 0 and n_pg % PAGE_CHUNK == 0
    steps_per_sample = n_pages // n_pg
    rows = 2 * tq
    srows = 2 * n_heads * dec_seq
    gw = n_hg * LANES

    def linear(b, h, i):
        return (b * n_groups + h) * n_q + i

    small = lambda n: pl.BlockSpec((1, n), lambda b, h, i, pt: (0, 0))
    new_spec = pl.BlockSpec((None, dec_seq, feat), lambda b, h, i, pt: (linear(b, h, i) // steps_per_sample, 0, 0))

    kern = functools.partial(_attn_kernel, n_pg=n_pg, steps_per_sample=steps_per_sample, tq=tq,
                             n_heads=n_heads, qk_dim=qk_dim, page=page, lam_init=lam_init)
    grid_spec = pltpu.PrefetchScalarGridSpec(
        num_scalar_prefetch=1,
        grid=(batch, n_groups, n_q),
        in_specs=[
            pl.BlockSpec((None, tq, gw), lambda b, h, i, pt: (b, i, h)),
            pl.BlockSpec((None, gw, seq), lambda b, h, i, pt: (b, h, 0), pipeline_mode=pl.Buffered(1)),
            pl.BlockSpec((None, seq, gw), lambda b, h, i, pt: (b, 0, h), pipeline_mode=pl.Buffered(1)),
            new_spec, new_spec, new_spec,
            pl.BlockSpec(memory_space=pl.ANY), pl.BlockSpec(memory_space=pl.ANY)]
        + [small(qk_dim)] * 4 + [small(v_dim)],
        out_specs=[pl.BlockSpec((None, tq, gw), lambda b, h, i, pt: (b, i, h)), new_spec],
        scratch_shapes=[
            pltpu.VMEM((n_hg, rows, 2 * qk_dim), BF16),
            pltpu.VMEM((n_hg, rows, tq), F32),
            pltpu.VMEM((n_hg, rows, tq), BF16),
            pltpu.VMEM((n_hg, rows, LANES), F32),
            pltpu.VMEM((n_hg, rows, LANES), F32),
            pltpu.VMEM((n_hg, rows, LANES), F32),
            pltpu.VMEM((n_hg, rows, v_dim), F32),
            pltpu.VMEM((srows, feat), BF16),
            pltpu.VMEM((feat, PAGE_CHUNK * page), BF16),
            pltpu.VMEM((PAGE_CHUNK * page, feat), BF16),
            pltpu.VMEM((srows, 1), F32),
            pltpu.VMEM((srows, 1), F32),
            pltpu.VMEM((srows, feat), F32),
            pltpu.VMEM((2, n_pg, feat, page), F32),
            pltpu.VMEM((2, n_pg, page * n_heads, v_dim), F32),
            pltpu.SemaphoreType.DMA((2,)),
        ],
    )
    return pl.pallas_call(
        kern,
        grid_spec=grid_spec,
        out_shape=[jax.ShapeDtypeStruct((batch, seq, feat), BF16),
                   jax.ShapeDtypeStruct((dec_batch, dec_seq, feat), F32)],
        compiler_params=pltpu.CompilerParams(
            dimension_semantics=("arbitrary", "arbitrary", "arbitrary"), vmem_limit_bytes=VMEM_LIMIT),
        name="attention",
    )(page_table, q, kt, v, qs, k_new, v_new, kt_pages, v_pages,
      lw['lq1'], lw['lk1'], lw['lq2'], lw['lk2'], lw['g_subln'])


def _mix_out_kernel(x_ref, oa_ref, hs_ref, yr_ref, ga_ref, gr_ref, p_ref,
                    wattn_ref, wrec_ref, wout_ref, gffn_ref, wg_ref, wu_ref, wd_ref,
                    gple_ref, wpg_ref, wpp_ref, gfin_ref, y_ref, *, ff_chunk, final_norm):
    dot = functools.partial(jnp.dot, preferred_element_type=F32)
    o_a = dot(oa_ref[...].astype(BF16), wattn_ref[...])
    o_r = dot((hs_ref[...] * jax.nn.gelu(yr_ref[...])).astype(BF16), wrec_ref[...])
    m = jax.nn.sigmoid(ga_ref[...]) * o_a + jax.nn.sigmoid(gr_ref[...]) * o_r
    h = x_ref[...] + dot(m.astype(BF16), wout_ref[...])

    u2 = _rms(h, gffn_ref[...]).astype(BF16)
    d_ff = wg_ref.shape[1]
    ffn = jnp.zeros_like(h)
    for c0 in range(0, d_ff, ff_chunk):
        c1 = min(c0 + ff_chunk, d_ff)
        hid = jax.nn.silu(dot(u2, wg_ref[:, c0:c1])) * dot(u2, wu_ref[:, c0:c1])
        ffn = ffn + dot(hid.astype(BF16), wd_ref[c0:c1, :])
    h = h + ffn

    g = jax.nn.sigmoid(dot(_rms(h, gple_ref[...]).astype(BF16), wpg_ref[...]))
    h = h + g * dot(p_ref[...].astype(BF16), wpp_ref[...])
    y_ref[...] = _rms(h, gfin_ref[...]) if final_norm else h


def _mix_out(x2d, oa, hs, yr, ga, gr, p2d, lw, g_final, *, tm, final_norm):
    n_tok, d_model = x2d.shape
    tok_spec = lambda w: pl.BlockSpec((tm, w), lambda i: (i, 0))
    weights = [lw['w_attn_br'], lw['w_rec_br'], lw['w_out'], lw['g_ffn'], lw['w_ffn_gate'], lw['w_ffn_up'],
               lw['w_ffn_down'], lw['g_ple'], lw['w_ple_gate'], lw['w_ple_proj'], g_final]
    kern = functools.partial(_mix_out_kernel, ff_chunk=1024, final_norm=final_norm)
    return pl.pallas_call(
        kern,
        grid=(n_tok // tm,),
        in_specs=[tok_spec(d_model), tok_spec(oa.shape[1]), tok_spec(hs.shape[1]), tok_spec(yr.shape[1]),
                  tok_spec(d_model), tok_spec(d_model), tok_spec(p2d.shape[1])]
        + [_resident(w.shape) for w in weights],
        out_specs=tok_spec(d_model),
        out_shape=jax.ShapeDtypeStruct((n_tok, d_model), F32),
        compiler_params=pltpu.CompilerParams(
            dimension_semantics=("arbitrary",), vmem_limit_bytes=VMEM_LIMIT),
        name="mix_out",
    )(x2d, oa, hs, yr, ga, gr, p2d, *weights)


def kernel(x_prompt, x_sample, p_prompt, p_sample, cache_k, cache_v, page_table, state_conv, state_h, g_mix, w_in, lambda_q1, lambda_k1, lambda_q2, lambda_k2, g_subln, w_attn_br, w_conv, b_conv, w_gate_a, b_gate_a, w_gate_x, b_gate_x, lru_lambda, w_rec_br, w_out, g_ffn, w_ffn_gate, w_ffn_up, w_ffn_down, g_ple, w_ple_gate, w_ple_proj, g_final):
    depth = w_in.shape[0]
    batch, seq, d_model = x_prompt.shape
    dec_batch, dec_seq, _ = x_sample.shape
    _, n_phys, page, n_heads, _, qk_dim = cache_k.shape
    v_dim = cache_v.shape[-1]
    width = w_conv.shape[-1]
    feat = n_heads * v_dim
    qk_scale = qk_dim ** -0.5 * math.log2(math.e)

    hp = x_prompt.reshape(batch * seq, d_model)
    hs = x_sample.reshape(dec_batch * dec_seq, d_model)
    outs = [[] for _ in range(8)]
    for l in range(depth):
        lam_init = 0.8 - 0.6 * math.exp(-0.3 * l)
        last = l == depth - 1
        lw = dict(
            g_mix=g_mix[l][None], w_in=w_in[l].astype(BF16),
            lq1=lambda_q1[l][None], lk1=lambda_k1[l][None], lq2=lambda_q2[l][None], lk2=lambda_k2[l][None],
            g_subln=g_subln[l][None], w_attn_br=w_attn_br[l].astype(BF16),
            w_conv=w_conv[l], b_conv=b_conv[l][None],
            w_gate=jnp.concatenate([w_gate_a[l], w_gate_x[l]], axis=-1).astype(BF16),
            b_gate=jnp.concatenate([b_gate_a[l], b_gate_x[l]], axis=-1)[:, None, :],
            lru_lambda=lru_lambda[l][None], w_rec_br=w_rec_br[l].astype(BF16), w_out=w_out[l].astype(BF16),
            g_ffn=g_ffn[l][None], w_ffn_gate=w_ffn_gate[l].astype(BF16), w_ffn_up=w_ffn_up[l].astype(BF16),
            w_ffn_down=w_ffn_down[l].astype(BF16), g_ple=g_ple[l][None],
            w_ple_gate=w_ple_gate[l].astype(BF16), w_ple_proj=w_ple_proj[l].astype(BF16))
        gfin = g_final[None]

        conv0 = jnp.zeros((batch, CONV_WIDTH - 1, width), F32)
        h0 = jnp.zeros((batch, 1, width), F32)
        q, kf, vf, kb, vb, rec, yr, ga, gr, cnew, hnew = _mix_in(
            hp, conv0, h0, lw, batch=batch, seq=seq, nb=1, tt=256, qk_scale=qk_scale, k_transposed=True)
        qs, kfs, vfs, _, _, recs, yrs, gas, grs, cnews, hnews = _mix_in(
            hs, state_conv[l], state_h[l][:, None, :], lw,
            batch=dec_batch, seq=dec_seq, nb=dec_batch, tt=dec_seq, qk_scale=qk_scale, k_transposed=False)

        kt_pages = jnp.transpose(cache_k[l], (0, 2, 3, 4, 1)).reshape(n_phys, feat, page)
        v_pages = cache_v[l].reshape(n_phys, page * n_heads, v_dim)
        oa, oas = _attention(
            q.reshape(batch, seq, feat), kb, vb.reshape(batch, seq, feat),
            qs.astype(F32).reshape(dec_batch, dec_seq, feat),
            kfs.reshape(dec_batch, dec_seq, feat), vfs.reshape(dec_batch, dec_seq, feat),
            kt_pages, v_pages, page_table, lw,
            n_heads=n_heads, qk_dim=qk_dim, lam_init=lam_init, tq=512, n_hg=2)

        hp = _mix_out(hp, oa.reshape(batch * seq, feat), rec, yr, ga, gr,
                      p_prompt[l].reshape(batch * seq, -1), lw, gfin, tm=256, final_norm=last)
        hs = _mix_out(hs, oas.reshape(dec_batch * dec_seq, feat), recs, yrs, gas, grs,
                      p_sample[l].reshape(dec_batch * dec_seq, -1), lw, gfin,
                      tm=dec_batch * dec_seq, final_norm=last)

        outs[0].append(jnp.transpose(kf.reshape(batch, n_heads, 2, qk_dim, seq), (0, 4, 1, 2, 3)))
        outs[1].append(vf.reshape(batch, seq, n_heads, v_dim))
        outs[2].append(cnew)
        outs[3].append(hnew.reshape(batch, width))
        outs[4].append(kfs.reshape(dec_batch, dec_seq, n_heads, 2, qk_dim))
        outs[5].append(vfs.reshape(dec_batch, dec_seq, n_heads, v_dim))
        outs[6].append(cnews)
        outs[7].append(hnews.reshape(dec_batch, width))

    y_prompt = hp.reshape(batch, seq, d_model)
    y_sample = hs.reshape(dec_batch, dec_seq, d_model)
    return (y_prompt, y_sample) + tuple(jnp.stack(o) for o in outs)
```

```python
import functools
import math

import jax
import jax.numpy as jnp
from jax import lax
from jax.experimental import pallas as pl
from jax.experimental.pallas import tpu as pltpu

F32 = jnp.float32
BF16 = jnp.bfloat16

EPS = 1e-6
LRU_C = 8.0
CONV_WIDTH = 4
N_LRU_BLOCKS = 8
SUBLANES = 8
LANES = 128
CONV_PAD = 8
PAGE_CHUNK = 16
FILL_COLS = 512
VMEM_LIMIT = 60 * 1024 * 1024


def _rms(x, g):
    ms = jnp.mean(x * x, axis=-1, keepdims=True)
    return x * lax.rsqrt(ms + EPS) * g


def _softplus(x):
    return jnp.maximum(x, 0.0) + jnp.log1p(jnp.exp(-jnp.abs(x)))


def _resident(shape):
    nd = len(shape)
    return pl.BlockSpec(shape, lambda *_: (0,) * nd, pipeline_mode=pl.Buffered(1))


def _mix_in_kernel(x_ref, cprev_ref, hprev_ref, gmix_ref, win_ref, wconv_ref, bconv_ref,
                   wgate_ref, bgate_ref, lam_ref,
                   q_ref, kf_ref, vf_ref, kb_ref, vb_ref, hs_ref, yr_ref, ga_ref, gr_ref,
                   cnew_ref, hnew_ref,
                   xbuf, abuf, ubuf, hstate, *, nb, tt, width, qk_scale, k_transposed):
    t = pl.program_id(1)
    tm = nb * tt
    lru_block = width // N_LRU_BLOCKS

    @pl.when(t == 0)
    def _():
        xbuf[:, CONV_PAD - (CONV_WIDTH - 1):CONV_PAD, :] = cprev_ref[...]
        hstate[...] = hprev_ref[...]

    u = _rms(x_ref[...], gmix_ref[...]).astype(BF16)

    def proj(i, c0=0, c1=width):
        return jnp.dot(u, win_ref[:, i * width + c0:i * width + c1], preferred_element_type=F32)

    def emit_q(c0, c1):
        q_ref[:, c0:c1] = (proj(0, c0, c1) * qk_scale).astype(BF16)

    def emit_k(c0, c1):
        if k_transposed:
            k = lax.dot_general(win_ref[:, width + c0:width + c1], u, (((0,), (1,)), ((), ())),
                                preferred_element_type=F32)
            kf_ref[c0:c1, :] = k
            kb_ref[c0:c1, :] = k.astype(BF16)
        else:
            k = proj(1, c0, c1)
            kf_ref[:, c0:c1] = k
            kb_ref[:, c0:c1] = k.astype(BF16)

    def emit_v(c0, c1):
        v = proj(2, c0, c1)
        vf_ref[:, c0:c1] = v
        vb_ref[:, c0:c1] = v.astype(BF16)

    def emit_raw(i, ref, c0, c1):
        ref[:, c0:c1] = proj(i, c0, c1)

    fillers = []
    for c0 in range(0, width, FILL_COLS):
        c1 = c0 + FILL_COLS
        fillers += [functools.partial(emit_q, c0, c1), functools.partial(emit_k, c0, c1),
                    functools.partial(emit_v, c0, c1), functools.partial(emit_raw, 4, yr_ref, c0, c1),
                    functools.partial(emit_raw, 5, ga_ref, c0, c1), functools.partial(emit_raw, 6, gr_ref, c0, c1)]
    w_conv, w_gate, w_scan = 4, 6, 1
    total_weight = w_conv + N_LRU_BLOCKS * w_gate + nb * (tt // SUBLANES) * w_scan
    done = [0, 0]

    def fill(weight):
        done[0] += weight
        target = -(-len(fillers) * done[0] // total_weight)
        while done[1] < target:
            fillers[done[1]]()
            done[1] += 1

    xbuf[:, CONV_PAD:CONV_PAD + tt, :] = proj(3).reshape(nb, tt, width)
    fill(w_conv)
    xc = bconv_ref[...].reshape(1, 1, width)
    for j in range(CONV_WIDTH):
        r0 = CONV_PAD - (CONV_WIDTH - 1) + j
        xc = xc + xbuf[:, r0:r0 + tt, :] * wconv_ref[j:j + 1, :].reshape(1, 1, width)
    tail = xbuf[:, CONV_PAD + tt - (CONV_WIDTH - 1):CONV_PAD + tt, :]
    cnew_ref[...] = tail
    xbuf[:, CONV_PAD - (CONV_WIDTH - 1):CONV_PAD, :] = tail

    xc = xc.reshape(tm, width)
    xcb = xc.astype(BF16)
    sp = _softplus(-lam_ref[...])
    for blk in range(N_LRU_BLOCKS):
        fill(w_gate)
        sl = slice(blk * lru_block, (blk + 1) * lru_block)
        g = jnp.dot(xcb[:, sl], wgate_ref[blk], preferred_element_type=F32) + bgate_ref[blk]
        r = jax.nn.sigmoid(g[:, :lru_block])
        ig = jax.nn.sigmoid(g[:, lru_block:])
        a = jnp.exp(-LRU_C * r * sp[:, sl])
        abuf[:, sl] = a
        ubuf[:, sl] = jnp.sqrt(1.0 - a * a) * (ig * xc[:, sl])

    n_groups = tt // SUBLANES
    row = lax.broadcasted_iota(jnp.int32, (SUBLANES, width), 0)

    for b in range(nb):
        h = hstate[b]
        for g in range(n_groups):
            fill(w_scan)
            r0 = (b * n_groups + g) * SUBLANES
            a = abuf[r0:r0 + SUBLANES, :]
            uu = ubuf[r0:r0 + SUBLANES, :]
            d = 1
            while d < SUBLANES:
                a_sh = jnp.where(row >= d, pltpu.roll(a, d, 0), 1.0)
                u_sh = jnp.where(row >= d, pltpu.roll(uu, d, 0), 0.0)
                uu = a * u_sh + uu
                a = a * a_sh
                d *= 2
            hb = uu + a * h
            hs_ref[r0:r0 + SUBLANES, :] = hb
            h = hb[SUBLANES - 1:SUBLANES, :]
        hstate[b] = h
    hnew_ref[...] = hstate[...]
    assert done[1] == len(fillers)


def _mix_in(x2d, conv_prev, h_prev, lw, *, batch, seq, nb, tt, qk_scale, k_transposed):
    n_tok, d_model = x2d.shape
    width = lw['w_conv'].shape[-1]
    n_t = seq // tt
    tm = nb * tt
    grid = (batch // nb, n_t)
    tok = lambda dt: jax.ShapeDtypeStruct((n_tok, width), dt)
    tok_spec = pl.BlockSpec((tm, width), lambda b, t: (b * n_t + t, 0))
    if k_transposed:
        assert nb == 1
        k_shape = lambda dt: jax.ShapeDtypeStruct((batch, width, seq), dt)
        k_spec = pl.BlockSpec((None, width, tt), lambda b, t: (b, 0, t))
    else:
        k_shape, k_spec = tok, tok_spec
    state_c = pl.BlockSpec((nb, CONV_WIDTH - 1, width), lambda b, t: (b, 0, 0))
    state_h = pl.BlockSpec((nb, 1, width), lambda b, t: (b, 0, 0))
    kern = functools.partial(_mix_in_kernel, nb=nb, tt=tt, width=width, qk_scale=qk_scale,
                             k_transposed=k_transposed)
    return pl.pallas_call(
        kern,
        grid=grid,
        in_specs=[
            pl.BlockSpec((tm, d_model), lambda b, t: (b * n_t + t, 0)),
            state_c, state_h,
            _resident((1, d_model)),
            _resident(lw['w_in'].shape),
            _resident((CONV_WIDTH, width)),
            _resident((1, width)),
            _resident(lw['w_gate'].shape),
            _resident(lw['b_gate'].shape),
            _resident((1, width)),
        ],
        out_specs=[tok_spec, k_spec, tok_spec, k_spec] + [tok_spec] * 5 + [state_c, state_h],
        out_shape=[tok(BF16), k_shape(F32), tok(F32), k_shape(BF16), tok(BF16),
                   tok(F32), tok(F32), tok(F32), tok(F32),
                   jax.ShapeDtypeStruct((batch, CONV_WIDTH - 1, width), F32),
                   jax.ShapeDtypeStruct((batch, 1, width), F32)],
        scratch_shapes=[
            pltpu.VMEM((nb, CONV_PAD + tt, width), F32),
            pltpu.VMEM((tm, width), F32),
            pltpu.VMEM((tm, width), F32),
            pltpu.VMEM((nb, 1, width), F32),
        ],
        compiler_params=pltpu.CompilerParams(
            dimension_semantics=("arbitrary", "arbitrary"), vmem_limit_bytes=VMEM_LIMIT),
        name="mix_in",
    )(x2d, conv_prev, h_prev, lw['g_mix'], lw['w_in'], lw['w_conv'], lw['b_conv'],
      lw['w_gate'], lw['b_gate'], lw['lru_lambda'])


def _diff_lambda(lq1_ref, lk1_ref, lq2_ref, lk2_ref, lam_init):
    s1 = jnp.sum(lq1_ref[...] * lk1_ref[...], axis=-1, keepdims=True)
    s2 = jnp.sum(lq2_ref[...] * lk2_ref[...], axis=-1, keepdims=True)
    return jnp.exp(s1) - jnp.exp(s2) + lam_init


def _head_out(o1, o2, lam, gsub, lam_init):
    od = o1 - lam * o2
    return _rms(od, gsub) * (1.0 - lam_init)


def _prompt_attn_body(i, q_ref, kt_ref, v_ref, lam, gsub_ref, o_ref,
                      qbd_scr, s_scr, p_scr, m_scr, l_scr, a_scr, acc_scr, *, tq, qk_dim, lam_init, row_chunk,
                      run_unmasked):
    tk = tq
    rows = 2 * tq
    n_hg = qbd_scr.shape[0]
    lane = lax.broadcasted_iota(jnp.int32, (tq, LANES), 1)
    for hh in range(n_hg):
        q = q_ref[:, hh * LANES:(hh + 1) * LANES]
        zero = jnp.zeros_like(q)
        qbd_scr[hh, 0:tq, :] = jnp.where(lane < qk_dim, q, zero)
        qbd_scr[hh, tq:rows, :] = jnp.where(lane >= qk_dim, q, zero)
    m_scr[...] = jnp.full(m_scr.shape, -jnp.inf, F32)
    l_scr[...] = jnp.zeros(l_scr.shape, F32)
    acc_scr[...] = jnp.zeros(acc_scr.shape, F32)

    def head_step(hh, k0, n_keys, row_blocks, key_offset=None, tick=None):
        hs = slice(hh * LANES, (hh + 1) * LANES)
        kt_blk = kt_ref[hs, pl.ds(k0, n_keys)]
        v_blk = v_ref[pl.ds(k0, n_keys), hs]
        for b0, nb in row_blocks:
            bs = slice(b0, b0 + nb)
            s_scr[hh, bs, 0:n_keys] = jnp.dot(qbd_scr[hh, bs, :], kt_blk, preferred_element_type=F32)
            for r0 in range(b0, b0 + nb, row_chunk):
                rs = slice(r0, r0 + row_chunk)
                s = s_scr[hh, rs, 0:n_keys]
                if key_offset is not None:
                    qpos = lax.broadcasted_iota(jnp.int32, (row_chunk, n_keys), 0) + (r0 % tq)
                    kpos = lax.broadcasted_iota(jnp.int32, (row_chunk, n_keys), 1) + key_offset
                    s = jnp.where(kpos <= qpos, s, -jnp.inf)
                tiles = [s[:, t * LANES:(t + 1) * LANES] for t in range(n_keys // LANES)]
                mx = functools.reduce(jnp.maximum, tiles)
                m_old = m_scr[hh, rs, :]
                m_new = jnp.maximum(m_old, jnp.max(mx, axis=-1, keepdims=True))
                alpha = jnp.exp2(m_old - m_new)
                ps = [jnp.exp2(t - m_new) for t in tiles]
                l_scr[hh, rs, :] = alpha * l_scr[hh, rs, :] + functools.reduce(jnp.add, ps)
                m_scr[hh, rs, :] = m_new
                a_scr[hh, rs, :] = alpha
                p_scr[hh, rs, 0:n_keys] = jnp.concatenate(ps, axis=1).astype(BF16)
                if tick is not None:
                    tick()
            pv = jnp.dot(p_scr[hh, bs, 0:n_keys], v_blk, preferred_element_type=F32)
            acc_scr[hh, bs, :] = a_scr[hh, bs, :] * acc_scr[hh, bs, :] + pv

    def unmasked(j, carry, fillers=()):
        k0 = j * tk if isinstance(j, int) else pl.multiple_of(j * tk, tk)
        tick = _spread(list(fillers), n_hg * (rows // row_chunk)) if fillers else None
        for hh in range(n_hg):
            head_step(hh, k0, tk, [(0, rows)], tick=tick)
        return carry

    run_unmasked(unmasked)
    d0 = pl.multiple_of(i * tk, tk)
    for hh in range(n_hg):
        head_step(hh, d0, tk, [(0, rows)], key_offset=0)
    for hh in range(n_hg):
        o = acc_scr[hh] / jnp.sum(l_scr[hh], axis=-1, keepdims=True)
        o_ref[:, hh * LANES:(hh + 1) * LANES] = _head_out(
            o[:tq], o[tq:], lam, gsub_ref[...], lam_init).astype(o_ref.dtype)


def _sample_attn_parts(q_ref, kn_ref, vn_ref, k_refs, v_refs, lam, gsub_ref, o_ref,
                       qbd_ref, kbuf, vbuf, m_ref, l_ref, acc_ref, *, n_heads, qk_dim, page, lam_init):
    tq, feat = q_ref.shape
    n_maps = 2 * n_heads
    rows = n_maps * tq
    v_dim = feat // n_heads

    def start():
        qt = jnp.concatenate([q_ref[...]] * n_maps, axis=0)
        r = lax.broadcasted_iota(jnp.int32, (rows, feat), 0)
        c = lax.broadcasted_iota(jnp.int32, (rows, feat), 1)
        qbd = jnp.where(r // tq == c // qk_dim, qt, 0.0).astype(BF16)
        qbd_ref[...] = qbd
        pad = jnp.zeros((page - tq, feat), F32)
        kn = jnp.concatenate([kn_ref[...], pad], axis=0).astype(BF16)
        vn = jnp.concatenate([vn_ref[...], pad], axis=0).astype(BF16)
        s = lax.dot_general(qbd, kn, (((1,), (1,)), ((), ())), preferred_element_type=F32)
        rr = lax.broadcasted_iota(jnp.int32, (rows, page), 0)
        cc = lax.broadcasted_iota(jnp.int32, (rows, page), 1)
        s = jnp.where(cc <= rr % tq, s, -jnp.inf)
        m = jnp.max(s, axis=-1, keepdims=True)
        p = jnp.exp2(s - m)
        m_ref[...] = m
        l_ref[...] = jnp.sum(p, axis=-1, keepdims=True)
        acc_ref[...] = jnp.dot(p.astype(BF16), vn, preferred_element_type=F32)

    def cast_page(c, g):
        src = c * PAGE_CHUNK + g
        kbuf[:, g * page:(g + 1) * page] = k_refs[src][...].astype(BF16)
        for h in range(n_heads):
            vh = v_refs[src][pl.ds(h, page, stride=n_heads), :]
            vbuf[g * page:(g + 1) * page, h * v_dim:(h + 1) * v_dim] = vh.astype(BF16)

    def reduce_chunk():
        s = jnp.dot(qbd_ref[...], kbuf[...], preferred_element_type=F32)
        m = m_ref[...]
        m_new = jnp.maximum(m, jnp.max(s, axis=-1, keepdims=True))
        alpha = jnp.exp2(m - m_new)
        p = jnp.exp2(s - m_new)
        l_ref[...] = alpha * l_ref[...] + jnp.sum(p, axis=-1, keepdims=True)
        acc_ref[...] = alpha * acc_ref[...] + jnp.dot(p.astype(BF16), vbuf[...], preferred_element_type=F32)
        m_ref[...] = m_new

    def finish():
        o = acc_ref[...] / l_ref[...]
        for h in range(n_heads):
            cols = slice(h * v_dim, (h + 1) * v_dim)
            o1 = o[(2 * h) * tq:(2 * h + 1) * tq, cols]
            o2 = o[(2 * h + 1) * tq:(2 * h + 2) * tq, cols]
            o_ref[:, cols] = _head_out(o1, o2, lam, gsub_ref[...], lam_init)

    return start, cast_page, reduce_chunk, finish


def _spread(fillers, n_slots):
    state = [0, 0]

    def tick():
        state[0] += 1
        target = -(-len(fillers) * state[0] // n_slots)
        while state[1] < target:
            fillers[state[1]]()
            state[1] += 1

    return tick


def _page_copies(pt_ref, kt_hbm, v_hbm, kpg, vpg, sem, step, slot, *, n_pg, steps_per_sample):
    batch_idx = step // steps_per_sample
    first = (step % steps_per_sample) * n_pg
    copies = []
    for g in range(n_pg):
        pid = pt_ref[batch_idx, first + g]
        copies.append(pltpu.make_async_copy(kt_hbm.at[pid], kpg.at[slot, g], sem.at[slot]))
        copies.append(pltpu.make_async_copy(v_hbm.at[pid], vpg.at[slot, g], sem.at[slot]))
    return copies


def _attn_kernel(pt_ref, q_ref, kt_ref, v_ref, qs_ref, kn_ref, vn_ref, kt_hbm, v_hbm,
                 lq1_ref, lk1_ref, lq2_ref, lk2_ref, gsub_ref, o_ref, os_ref,
                 qbd_scr, s_scr, p_scr, m_scr, l_scr, a_scr, acc_scr,
                 sqbd_scr, kbuf, vbuf, sm_scr, sl_scr, sacc_scr, kpg, vpg, page_sem,
                 *, n_pg, steps_per_sample, tq, n_heads, qk_dim, page, lam_init):
    i = pl.program_id(2)
    step = (pl.program_id(0) * pl.num_programs(1) + pl.program_id(1)) * pl.num_programs(2) + i
    total_steps = pl.num_programs(0) * pl.num_programs(1) * pl.num_programs(2)
    slot = step % 2
    copies = functools.partial(_page_copies, pt_ref, kt_hbm, v_hbm, kpg, vpg, page_sem,
                               n_pg=n_pg, steps_per_sample=steps_per_sample)

    @pl.when(step == 0)
    def _():
        for c in copies(step, slot):
            c.start()

    @pl.when(step + 1 < total_steps)
    def _():
        for c in copies(step + 1, 1 - slot):
            c.start()

    for c in copies(step, slot):
        c.wait()
    k_refs = [kpg.at[slot, g] for g in range(n_pg)]
    v_refs = [vpg.at[slot, g] for g in range(n_pg)]
    lam = _diff_lambda(lq1_ref, lk1_ref, lq2_ref, lk2_ref, lam_init)
    sample_start, cast_page, reduce_chunk, sample_finish = _sample_attn_parts(
        qs_ref, kn_ref, vn_ref, k_refs, v_refs, lam, gsub_ref, os_ref,
        sqbd_scr, kbuf, vbuf, sm_scr, sl_scr, sacc_scr,
        n_heads=n_heads, qk_dim=qk_dim, page=page, lam_init=lam_init)
    n_chunks = n_pg // PAGE_CHUNK
    sub_step = step % steps_per_sample
    pl.when(sub_step == 0)(sample_start)

    def run_unmasked(key_block):
        casts = lambda c: [functools.partial(cast_page, c, g) for g in range(PAGE_CHUNK)]
        for n_fat in range(n_chunks + 1):
            def branch(n_fat=n_fat):
                for c in range(n_fat):
                    key_block(c, 0, casts(c))
                    reduce_chunk()
                for c in range(n_fat, n_chunks):
                    for cast in casts(c):
                        cast()
                    reduce_chunk()
                if n_fat == n_chunks:
                    lax.fori_loop(n_chunks, i, key_block, 0)
            pl.when(i >= n_fat if n_fat == n_chunks else i == n_fat)(branch)

    _prompt_attn_body(i, q_ref, kt_ref, v_ref, lam, gsub_ref, o_ref,
                      qbd_scr, s_scr, p_scr, m_scr, l_scr, a_scr, acc_scr,
                      tq=tq, qk_dim=qk_dim, lam_init=lam_init, row_chunk=32, run_unmasked=run_unmasked)
    pl.when(sub_step == steps_per_sample - 1)(sample_finish)


def _attention(q, kt, v, qs, k_new, v_new, kt_pages, v_pages, page_table, lw, *,
               n_heads, qk_dim, lam_init, tq, n_hg):
    batch, seq, feat = q.shape
    dec_batch, dec_seq, _ = qs.shape
    page = kt_pages.shape[-1]
    n_pages = page_table.shape[1]
    v_dim = feat // n_heads
    n_groups = n_heads // n_hg
    n_q = seq // tq
    total_steps = batch * n_groups * n_q
    assert 2 * qk_dim == LANES and v_dim == LANES and page == LANES and n_heads % n_hg == 0
    assert (dec_batch * n_pages) % total_steps == 0
    n_pg = dec_batch * n_pages // total_steps
    assert n_pages % n_pg == 0 and n_pg % PAGE_CHUNK == 0
    steps_per_sample = n_pages // n_pg
    rows = 2 * tq
    srows = 2 * n_heads * dec_seq
    gw = n_hg * LANES

    def linear(b, h, i):
        return (b * n_groups + h) * n_q + i

    small = lambda n: pl.BlockSpec((1, n), lambda b, h, i, pt: (0, 0))
    new_spec = pl.BlockSpec((None, dec_seq, feat), lambda b, h, i, pt: (linear(b, h, i) // steps_per_sample, 0, 0))

    kern = functools.partial(_attn_kernel, n_pg=n_pg, steps_per_sample=steps_per_sample, tq=tq,
                             n_heads=n_heads, qk_dim=qk_dim, page=page, lam_init=lam_init)
    grid_spec = pltpu.PrefetchScalarGridSpec(
        num_scalar_prefetch=1,
        grid=(batch, n_groups, n_q),
        in_specs=[
            pl.BlockSpec((None, tq, gw), lambda b, h, i, pt: (b, i, h)),
            pl.BlockSpec((None, gw, seq), lambda b, h, i, pt: (b, h, 0), pipeline_mode=pl.Buffered(1)),
            pl.BlockSpec((None, seq, gw), lambda b, h, i, pt: (b, 0, h), pipeline_mode=pl.Buffered(1)),
            new_spec, new_spec, new_spec,
            pl.BlockSpec(memory_space=pl.ANY), pl.BlockSpec(memory_space=pl.ANY)]
        + [small(qk_dim)] * 4 + [small(v_dim)],
        out_specs=[pl.BlockSpec((None, tq, gw), lambda b, h, i, pt: (b, i, h)), new_spec],
        scratch_shapes=[
            pltpu.VMEM((n_hg, rows, 2 * qk_dim), BF16),
            pltpu.VMEM((n_hg, rows, tq), F32),
            pltpu.VMEM((n_hg, rows, tq), BF16),
            pltpu.VMEM((n_hg, rows, LANES), F32),
            pltpu.VMEM((n_hg, rows, LANES), F32),
            pltpu.VMEM((n_hg, rows, LANES), F32),
            pltpu.VMEM((n_hg, rows, v_dim), F32),
            pltpu.VMEM((srows, feat), BF16),
            pltpu.VMEM((feat, PAGE_CHUNK * page), BF16),
            pltpu.VMEM((PAGE_CHUNK * page, feat), BF16),
            pltpu.VMEM((srows, 1), F32),
            pltpu.VMEM((srows, 1), F32),
            pltpu.VMEM((srows, feat), F32),
            pltpu.VMEM((2, n_pg, feat, page), F32),
            pltpu.VMEM((2, n_pg, page * n_heads, v_dim), F32),
            pltpu.SemaphoreType.DMA((2,)),
        ],
    )
    return pl.pallas_call(
        kern,
        grid_spec=grid_spec,
        out_shape=[jax.ShapeDtypeStruct((batch, seq, feat), BF16),
                   jax.ShapeDtypeStruct((dec_batch, dec_seq, feat), F32)],
        compiler_params=pltpu.CompilerParams(
            dimension_semantics=("arbitrary", "arbitrary", "arbitrary"), vmem_limit_bytes=VMEM_LIMIT),
        name="attention",
    )(page_table, q, kt, v, qs, k_new, v_new, kt_pages, v_pages,
      lw['lq1'], lw['lk1'], lw['lq2'], lw['lk2'], lw['g_subln'])


def _mix_out_kernel(x_ref, oa_ref, hs_ref, yr_ref, ga_ref, gr_ref, p_ref,
                    wattn_ref, wrec_ref, wout_ref, gffn_ref, wg_ref, wu_ref, wd_ref,
                    gple_ref, wpg_ref, wpp_ref, gfin_ref, y_ref, *, ff_chunk, final_norm):
    dot = functools.partial(jnp.dot, preferred_element_type=F32)
    o_a = dot(oa_ref[...].astype(BF16), wattn_ref[...])
    o_r = dot((hs_ref[...] * jax.nn.gelu(yr_ref[...])).astype(BF16), wrec_ref[...])
    m = jax.nn.sigmoid(ga_ref[...]) * o_a + jax.nn.sigmoid(gr_ref[...]) * o_r
    h = x_ref[...] + dot(m.astype(BF16), wout_ref[...])

    u2 = _rms(h, gffn_ref[...]).astype(BF16)
    d_ff = wg_ref.shape[1]
    ffn = jnp.zeros_like(h)
    for c0 in range(0, d_ff, ff_chunk):
        c1 = min(c0 + ff_chunk, d_ff)
        hid = jax.nn.silu(dot(u2, wg_ref[:, c0:c1])) * dot(u2, wu_ref[:, c0:c1])
        ffn = ffn + dot(hid.astype(BF16), wd_ref[c0:c1, :])
    h = h + ffn

    g = jax.nn.sigmoid(dot(_rms(h, gple_ref[...]).astype(BF16), wpg_ref[...]))
    h = h + g * dot(p_ref[...].astype(BF16), wpp_ref[...])
    y_ref[...] = _rms(h, gfin_ref[...]) if final_norm else h


def _mix_out(x2d, oa, hs, yr, ga, gr, p2d, lw, g_final, *, tm, final_norm):
    n_tok, d_model = x2d.shape
    tok_spec = lambda w: pl.BlockSpec((tm, w), lambda i: (i, 0))
    weights = [lw['w_attn_br'], lw['w_rec_br'], lw['w_out'], lw['g_ffn'], lw['w_ffn_gate'], lw['w_ffn_up'],
               lw['w_ffn_down'], lw['g_ple'], lw['w_ple_gate'], lw['w_ple_proj'], g_final]
    kern = functools.partial(_mix_out_kernel, ff_chunk=1024, final_norm=final_norm)
    return pl.pallas_call(
        kern,
        grid=(n_tok // tm,),
        in_specs=[tok_spec(d_model), tok_spec(oa.shape[1]), tok_spec(hs.shape[1]), tok_spec(yr.shape[1]),
                  tok_spec(d_model), tok_spec(d_model), tok_spec(p2d.shape[1])]
        + [_resident(w.shape) for w in weights],
        out_specs=tok_spec(d_model),
        out_shape=jax.ShapeDtypeStruct((n_tok, d_model), F32),
        compiler_params=pltpu.CompilerParams(
            dimension_semantics=("arbitrary",), vmem_limit_bytes=VMEM_LIMIT),
        name="mix_out",
    )(x2d, oa, hs, yr, ga, gr, p2d, *weights)


def kernel(x_prompt, x_sample, p_prompt, p_sample, cache_k, cache_v, page_table, state_conv, state_h, g_mix, w_in, lambda_q1, lambda_k1, lambda_q2, lambda_k2, g_subln, w_attn_br, w_conv, b_conv, w_gate_a, b_gate_a, w_gate_x, b_gate_x, lru_lambda, w_rec_br, w_out, g_ffn, w_ffn_gate, w_ffn_up, w_ffn_down, g_ple, w_ple_gate, w_ple_proj, g_final):
    depth = w_in.shape[0]
    batch, seq, d_model = x_prompt.shape
    dec_batch, dec_seq, _ = x_sample.shape
    _, n_phys, page, n_heads, _, qk_dim = cache_k.shape
    v_dim = cache_v.shape[-1]
    width = w_conv.shape[-1]
    feat = n_heads * v_dim
    qk_scale = qk_dim ** -0.5 * math.log2(math.e)

    hp = x_prompt.reshape(batch * seq, d_model)
    hs = x_sample.reshape(dec_batch * dec_seq, d_model)
    outs = [[] for _ in range(8)]
    for l in range(depth):
        lam_init = 0.8 - 0.6 * math.exp(-0.3 * l)
        last = l == depth - 1
        lw = dict(
            g_mix=g_mix[l][None], w_in=w_in[l].astype(BF16),
            lq1=lambda_q1[l][None], lk1=lambda_k1[l][None], lq2=lambda_q2[l][None], lk2=lambda_k2[l][None],
            g_subln=g_subln[l][None], w_attn_br=w_attn_br[l].astype(BF16),
            w_conv=w_conv[l], b_conv=b_conv[l][None],
            w_gate=jnp.concatenate([w_gate_a[l], w_gate_x[l]], axis=-1).astype(BF16),
            b_gate=jnp.concatenate([b_gate_a[l], b_gate_x[l]], axis=-1)[:, None, :],
            lru_lambda=lru_lambda[l][None], w_rec_br=w_rec_br[l].astype(BF16), w_out=w_out[l].astype(BF16),
            g_ffn=g_ffn[l][None], w_ffn_gate=w_ffn_gate[l].astype(BF16), w_ffn_up=w_ffn_up[l].astype(BF16),
            w_ffn_down=w_ffn_down[l].astype(BF16), g_ple=g_ple[l][None],
            w_ple_gate=w_ple_gate[l].astype(BF16), w_ple_proj=w_ple_proj[l].astype(BF16))
        gfin = g_final[None]

        conv0 = jnp.zeros((batch, CONV_WIDTH - 1, width), F32)
        h0 = jnp.zeros((batch, 1, width), F32)
        q, kf, vf, kb, vb, rec, yr, ga, gr, cnew, hnew = _mix_in(
            hp, conv0, h0, lw, batch=batch, seq=seq, nb=1, tt=256, qk_scale=qk_scale, k_transposed=True)
        qs, kfs, vfs, _, _, recs, yrs, gas, grs, cnews, hnews = _mix_in(
            hs, state_conv[l], state_h[l][:, None, :], lw,
            batch=dec_batch, seq=dec_seq, nb=dec_batch, tt=dec_seq, qk_scale=qk_scale, k_transposed=False)

        kt_pages = jnp.transpose(cache_k[l], (0, 2, 3, 4, 1)).reshape(n_phys, feat, page)
        v_pages = cache_v[l].reshape(n_phys, page * n_heads, v_dim)
        oa, oas = _attention(
            q.reshape(batch, seq, feat), kb, vb.reshape(batch, seq, feat),
            qs.astype(F32).reshape(dec_batch, dec_seq, feat),
            kfs.reshape(dec_batch, dec_seq, feat), vfs.reshape(dec_batch, dec_seq, feat),
            kt_pages, v_pages, page_table, lw,
            n_heads=n_heads, qk_dim=qk_dim, lam_init=lam_init, tq=512, n_hg=2)

        hp = _mix_out(hp, oa.reshape(batch * seq, feat), rec, yr, ga, gr,
                      p_prompt[l].reshape(batch * seq, -1), lw, gfin, tm=256, final_norm=last)
        hs = _mix_out(hs, oas.reshape(dec_batch * dec_seq, feat), recs, yrs, gas, grs,
                      p_sample[l].reshape(dec_batch * dec_seq, -1), lw, gfin,
                      tm=dec_batch * dec_seq, final_norm=last)

        outs[0].append(jnp.transpose(kf.reshape(batch, n_heads, 2, qk_dim, seq), (0, 4, 1, 2, 3)))
        outs[1].append(vf.reshape(batch, seq, n_heads, v_dim))
        outs[2].append(cnew)
        outs[3].append(hnew.reshape(batch, width))
        outs[4].append(kfs.reshape(dec_batch, dec_seq, n_heads, 2, qk_dim))
        outs[5].append(vfs.reshape(dec_batch, dec_seq, n_heads, v_dim))
        outs[6].append(cnews)
        outs[7].append(hnews.reshape(dec_batch, width))

    y_prompt = hp.reshape(batch, seq, d_model)
    y_sample = hs.reshape(dec_batch, dec_seq, d_model)
    return (y_prompt, y_sample) + tuple(jnp.stack(o) for o in outs)
```

```python
import functools
import math

import jax
import jax.numpy as jnp
from jax import lax
from jax.experimental import pallas as pl
from jax.experimental.pallas import tpu as pltpu

F32 = jnp.float32
BF16 = jnp.bfloat16

EPS = 1e-6
LRU_C = 8.0
CONV_WIDTH = 4
N_LRU_BLOCKS = 8
SUBLANES = 8
LANES = 128
CONV_PAD = 8
PAGE_CHUNK = 16
CAST_SPREAD = 2
FILL_COLS = 512
VMEM_LIMIT = 60 * 1024 * 1024


def _rms(x, g):
    ms = jnp.mean(x * x, axis=-1, keepdims=True)
    return x * lax.rsqrt(ms + EPS) * g


def _softplus(x):
    return jnp.maximum(x, 0.0) + jnp.log1p(jnp.exp(-jnp.abs(x)))


def _resident(shape):
    nd = len(shape)
    return pl.BlockSpec(shape, lambda *_: (0,) * nd, pipeline_mode=pl.Buffered(1))


def _mix_in_kernel(x_ref, cprev_ref, hprev_ref, gmix_ref, win_ref, wconv_ref, bconv_ref,
                   wgate_ref, bgate_ref, lam_ref,
                   q_ref, kf_ref, vf_ref, kb_ref, vb_ref, hs_ref, yr_ref, ga_ref, gr_ref,
                   cnew_ref, hnew_ref,
                   xbuf, abuf, ubuf, hstate, *, nb, tt, width, qk_scale, k_transposed):
    t = pl.program_id(1)
    tm = nb * tt
    lru_block = width // N_LRU_BLOCKS

    @pl.when(t == 0)
    def _():
        xbuf[:, CONV_PAD - (CONV_WIDTH - 1):CONV_PAD, :] = cprev_ref[...]
        hstate[...] = hprev_ref[...]

    u = _rms(x_ref[...], gmix_ref[...]).astype(BF16)

    def proj(i, c0=0, c1=width):
        return jnp.dot(u, win_ref[:, i * width + c0:i * width + c1], preferred_element_type=F32)

    def emit_q(c0, c1):
        q_ref[:, c0:c1] = (proj(0, c0, c1) * qk_scale).astype(BF16)

    def emit_k(c0, c1):
        if k_transposed:
            k = lax.dot_general(win_ref[:, width + c0:width + c1], u, (((0,), (1,)), ((), ())),
                                preferred_element_type=F32)
            kf_ref[c0:c1, :] = k
            kb_ref[c0:c1, :] = k.astype(BF16)
        else:
            k = proj(1, c0, c1)
            kf_ref[:, c0:c1] = k
            kb_ref[:, c0:c1] = k.astype(BF16)

    def emit_v(c0, c1):
        v = proj(2, c0, c1)
        vf_ref[:, c0:c1] = v
        vb_ref[:, c0:c1] = v.astype(BF16)

    def emit_raw(i, ref, c0, c1):
        ref[:, c0:c1] = proj(i, c0, c1)

    fillers = []
    for c0 in range(0, width, FILL_COLS):
        c1 = c0 + FILL_COLS
        fillers += [functools.partial(emit_q, c0, c1), functools.partial(emit_k, c0, c1),
                    functools.partial(emit_v, c0, c1), functools.partial(emit_raw, 4, yr_ref, c0, c1),
                    functools.partial(emit_raw, 5, ga_ref, c0, c1), functools.partial(emit_raw, 6, gr_ref, c0, c1)]
    w_conv, w_gate, w_scan = 4, 6, 1
    total_weight = w_conv + N_LRU_BLOCKS * w_gate + nb * (tt // SUBLANES) * w_scan
    done = [0, 0]

    def fill(weight):
        done[0] += weight
        target = -(-len(fillers) * done[0] // total_weight)
        while done[1] < target:
            fillers[done[1]]()
            done[1] += 1

    xbuf[:, CONV_PAD:CONV_PAD + tt, :] = proj(3).reshape(nb, tt, width)
    fill(w_conv)
    xc = bconv_ref[...].reshape(1, 1, width)
    for j in range(CONV_WIDTH):
        r0 = CONV_PAD - (CONV_WIDTH - 1) + j
        xc = xc + xbuf[:, r0:r0 + tt, :] * wconv_ref[j:j + 1, :].reshape(1, 1, width)
    tail = xbuf[:, CONV_PAD + tt - (CONV_WIDTH - 1):CONV_PAD + tt, :]
    cnew_ref[...] = tail
    xbuf[:, CONV_PAD - (CONV_WIDTH - 1):CONV_PAD, :] = tail

    xc = xc.reshape(tm, width)
    xcb = xc.astype(BF16)
    sp = _softplus(-lam_ref[...])
    for blk in range(N_LRU_BLOCKS):
        fill(w_gate)
        sl = slice(blk * lru_block, (blk + 1) * lru_block)
        g = jnp.dot(xcb[:, sl], wgate_ref[blk], preferred_element_type=F32) + bgate_ref[blk]
        r = jax.nn.sigmoid(g[:, :lru_block])
        ig = jax.nn.sigmoid(g[:, lru_block:])
        a = jnp.exp(-LRU_C * r * sp[:, sl])
        abuf[:, sl] = a
        ubuf[:, sl] = jnp.sqrt(1.0 - a * a) * (ig * xc[:, sl])

    n_groups = tt // SUBLANES
    row = lax.broadcasted_iota(jnp.int32, (SUBLANES, width), 0)

    for b in range(nb):
        h = hstate[b]
        for g in range(n_groups):
            fill(w_scan)
            r0 = (b * n_groups + g) * SUBLANES
            a = abuf[r0:r0 + SUBLANES, :]
            uu = ubuf[r0:r0 + SUBLANES, :]
            d = 1
            while d < SUBLANES:
                a_sh = jnp.where(row >= d, pltpu.roll(a, d, 0), 1.0)
                u_sh = jnp.where(row >= d, pltpu.roll(uu, d, 0), 0.0)
                uu = a * u_sh + uu
                a = a * a_sh
                d *= 2
            hb = uu + a * h
            hs_ref[r0:r0 + SUBLANES, :] = hb
            h = hb[SUBLANES - 1:SUBLANES, :]
        hstate[b] = h
    hnew_ref[...] = hstate[...]
    assert done[1] == len(fillers)


def _mix_in(x2d, conv_prev, h_prev, lw, *, batch, seq, nb, tt, qk_scale, k_transposed):
    n_tok, d_model = x2d.shape
    width = lw['w_conv'].shape[-1]
    n_t = seq // tt
    tm = nb * tt
    grid = (batch // nb, n_t)
    tok = lambda dt: jax.ShapeDtypeStruct((n_tok, width), dt)
    tok_spec = pl.BlockSpec((tm, width), lambda b, t: (b * n_t + t, 0))
    if k_transposed:
        assert nb == 1
        k_shape = lambda dt: jax.ShapeDtypeStruct((batch, width, seq), dt)
        k_spec = pl.BlockSpec((None, width, tt), lambda b, t: (b, 0, t))
    else:
        k_shape, k_spec = tok, tok_spec
    state_c = pl.BlockSpec((nb, CONV_WIDTH - 1, width), lambda b, t: (b, 0, 0))
    state_h = pl.BlockSpec((nb, 1, width), lambda b, t: (b, 0, 0))
    kern = functools.partial(_mix_in_kernel, nb=nb, tt=tt, width=width, qk_scale=qk_scale,
                             k_transposed=k_transposed)
    return pl.pallas_call(
        kern,
        grid=grid,
        in_specs=[
            pl.BlockSpec((tm, d_model), lambda b, t: (b * n_t + t, 0)),
            state_c, state_h,
            _resident((1, d_model)),
            _resident(lw['w_in'].shape),
            _resident((CONV_WIDTH, width)),
            _resident((1, width)),
            _resident(lw['w_gate'].shape),
            _resident(lw['b_gate'].shape),
            _resident((1, width)),
        ],
        out_specs=[tok_spec, k_spec, tok_spec, k_spec] + [tok_spec] * 5 + [state_c, state_h],
        out_shape=[tok(BF16), k_shape(F32), tok(F32), k_shape(BF16), tok(BF16),
                   tok(F32), tok(F32), tok(F32), tok(F32),
                   jax.ShapeDtypeStruct((batch, CONV_WIDTH - 1, width), F32),
                   jax.ShapeDtypeStruct((batch, 1, width), F32)],
        scratch_shapes=[
            pltpu.VMEM((nb, CONV_PAD + tt, width), F32),
            pltpu.VMEM((tm, width), F32),
            pltpu.VMEM((tm, width), F32),
            pltpu.VMEM((nb, 1, width), F32),
        ],
        compiler_params=pltpu.CompilerParams(
            dimension_semantics=("arbitrary", "arbitrary"), vmem_limit_bytes=VMEM_LIMIT),
        name="mix_in",
    )(x2d, conv_prev, h_prev, lw['g_mix'], lw['w_in'], lw['w_conv'], lw['b_conv'],
      lw['w_gate'], lw['b_gate'], lw['lru_lambda'])


def _diff_lambda(lq1_ref, lk1_ref, lq2_ref, lk2_ref, lam_init):
    s1 = jnp.sum(lq1_ref[...] * lk1_ref[...], axis=-1, keepdims=True)
    s2 = jnp.sum(lq2_ref[...] * lk2_ref[...], axis=-1, keepdims=True)
    return jnp.exp(s1) - jnp.exp(s2) + lam_init


def _head_out(o1, o2, lam, gsub, lam_init):
    od = o1 - lam * o2
    return _rms(od, gsub) * (1.0 - lam_init)


def _prompt_attn_body(i, q_ref, kt_ref, v_ref, lam, gsub_ref, o_ref,
                      qbd_scr, s_scr, p_scr, m_scr, l_scr, a_scr, acc_scr, *, tq, qk_dim, lam_init, row_chunk,
                      run_unmasked):
    tk = tq
    rows = 2 * tq
    n_hg = qbd_scr.shape[0]
    lane = lax.broadcasted_iota(jnp.int32, (tq, LANES), 1)
    for hh in range(n_hg):
        q = q_ref[:, hh * LANES:(hh + 1) * LANES]
        zero = jnp.zeros_like(q)
        qbd_scr[hh, 0:tq, :] = jnp.where(lane < qk_dim, q, zero)
        qbd_scr[hh, tq:rows, :] = jnp.where(lane >= qk_dim, q, zero)
    m_scr[...] = jnp.full(m_scr.shape, -jnp.inf, F32)
    l_scr[...] = jnp.zeros(l_scr.shape, F32)
    acc_scr[...] = jnp.zeros(acc_scr.shape, F32)

    def head_step(hh, k0, n_keys, row_blocks, key_offset=None, tick=None):
        hs = slice(hh * LANES, (hh + 1) * LANES)
        kt_blk = kt_ref[hs, pl.ds(k0, n_keys)]
        v_blk = v_ref[pl.ds(k0, n_keys), hs]
        for b0, nb in row_blocks:
            bs = slice(b0, b0 + nb)
            s_scr[hh, bs, 0:n_keys] = jnp.dot(qbd_scr[hh, bs, :], kt_blk, preferred_element_type=F32)
            for r0 in range(b0, b0 + nb, row_chunk):
                rs = slice(r0, r0 + row_chunk)
                s = s_scr[hh, rs, 0:n_keys]
                if key_offset is not None:
                    qpos = lax.broadcasted_iota(jnp.int32, (row_chunk, n_keys), 0) + (r0 % tq)
                    kpos = lax.broadcasted_iota(jnp.int32, (row_chunk, n_keys), 1) + key_offset
                    s = jnp.where(kpos <= qpos, s, -jnp.inf)
                tiles = [s[:, t * LANES:(t + 1) * LANES] for t in range(n_keys // LANES)]
                mx = functools.reduce(jnp.maximum, tiles)
                m_old = m_scr[hh, rs, :]
                m_new = jnp.maximum(m_old, jnp.max(mx, axis=-1, keepdims=True))
                alpha = jnp.exp2(m_old - m_new)
                ps = [jnp.exp2(t - m_new) for t in tiles]
                l_scr[hh, rs, :] = alpha * l_scr[hh, rs, :] + functools.reduce(jnp.add, ps)
                m_scr[hh, rs, :] = m_new
                a_scr[hh, rs, :] = alpha
                p_scr[hh, rs, 0:n_keys] = jnp.concatenate(ps, axis=1).astype(BF16)
                if tick is not None:
                    tick()
            pv = jnp.dot(p_scr[hh, bs, 0:n_keys], v_blk, preferred_element_type=F32)
            acc_scr[hh, bs, :] = a_scr[hh, bs, :] * acc_scr[hh, bs, :] + pv

    def unmasked(j, carry, fillers=()):
        k0 = j * tk if isinstance(j, int) else pl.multiple_of(j * tk, tk)
        tick = _spread(list(fillers), n_hg * (rows // row_chunk)) if fillers else None
        for hh in range(n_hg):
            head_step(hh, k0, tk, [(0, rows)], tick=tick)
        return carry

    run_unmasked(unmasked)
    d0 = pl.multiple_of(i * tk, tk)
    for hh in range(n_hg):
        head_step(hh, d0, tk, [(0, rows)], key_offset=0)
    for hh in range(n_hg):
        o = acc_scr[hh] / jnp.sum(l_scr[hh], axis=-1, keepdims=True)
        o_ref[:, hh * LANES:(hh + 1) * LANES] = _head_out(
            o[:tq], o[tq:], lam, gsub_ref[...], lam_init).astype(o_ref.dtype)


def _sample_attn_parts(q_ref, kn_ref, vn_ref, k_refs, v_refs, lam, gsub_ref, o_ref,
                       qbd_ref, kbuf, vbuf, m_ref, l_ref, acc_ref, *, n_heads, qk_dim, page, lam_init):
    tq, feat = q_ref.shape
    n_maps = 2 * n_heads
    rows = n_maps * tq
    v_dim = feat // n_heads

    def start():
        qt = jnp.concatenate([q_ref[...]] * n_maps, axis=0)
        r = lax.broadcasted_iota(jnp.int32, (rows, feat), 0)
        c = lax.broadcasted_iota(jnp.int32, (rows, feat), 1)
        qbd = jnp.where(r // tq == c // qk_dim, qt, 0.0).astype(BF16)
        qbd_ref[...] = qbd
        pad = jnp.zeros((page - tq, feat), F32)
        kn = jnp.concatenate([kn_ref[...], pad], axis=0).astype(BF16)
        vn = jnp.concatenate([vn_ref[...], pad], axis=0).astype(BF16)
        s = lax.dot_general(qbd, kn, (((1,), (1,)), ((), ())), preferred_element_type=F32)
        rr = lax.broadcasted_iota(jnp.int32, (rows, page), 0)
        cc = lax.broadcasted_iota(jnp.int32, (rows, page), 1)
        s = jnp.where(cc <= rr % tq, s, -jnp.inf)
        m = jnp.max(s, axis=-1, keepdims=True)
        p = jnp.exp2(s - m)
        m_ref[...] = m
        l_ref[...] = jnp.sum(p, axis=-1, keepdims=True)
        acc_ref[...] = jnp.dot(p.astype(BF16), vn, preferred_element_type=F32)

    def cast_page(c, g):
        src = c * PAGE_CHUNK + g
        kbuf[:, g * page:(g + 1) * page] = k_refs[src][...].astype(BF16)
        for h in range(n_heads):
            vh = v_refs[src][pl.ds(h, page, stride=n_heads), :]
            vbuf[g * page:(g + 1) * page, h * v_dim:(h + 1) * v_dim] = vh.astype(BF16)

    def reduce_chunk():
        s = jnp.dot(qbd_ref[...], kbuf[...], preferred_element_type=F32)
        m = m_ref[...]
        m_new = jnp.maximum(m, jnp.max(s, axis=-1, keepdims=True))
        alpha = jnp.exp2(m - m_new)
        p = jnp.exp2(s - m_new)
        l_ref[...] = alpha * l_ref[...] + jnp.sum(p, axis=-1, keepdims=True)
        acc_ref[...] = alpha * acc_ref[...] + jnp.dot(p.astype(BF16), vbuf[...], preferred_element_type=F32)
        m_ref[...] = m_new

    def finish():
        o = acc_ref[...] / l_ref[...]
        for h in range(n_heads):
            cols = slice(h * v_dim, (h + 1) * v_dim)
            o1 = o[(2 * h) * tq:(2 * h + 1) * tq, cols]
            o2 = o[(2 * h + 1) * tq:(2 * h + 2) * tq, cols]
            o_ref[:, cols] = _head_out(o1, o2, lam, gsub_ref[...], lam_init)

    return start, cast_page, reduce_chunk, finish


def _spread(fillers, n_slots):
    state = [0, 0]

    def tick():
        state[0] += 1
        target = -(-len(fillers) * state[0] // n_slots)
        while state[1] < target:
            fillers[state[1]]()
            state[1] += 1

    return tick


def _page_copies(pt_ref, kt_hbm, v_hbm, kpg, vpg, sem, step, slot, *, n_pg, steps_per_sample):
    batch_idx = step // steps_per_sample
    first = (step % steps_per_sample) * n_pg
    copies = []
    for g in range(n_pg):
        pid = pt_ref[batch_idx, first + g]
        copies.append(pltpu.make_async_copy(kt_hbm.at[pid], kpg.at[slot, g], sem.at[slot]))
        copies.append(pltpu.make_async_copy(v_hbm.at[pid], vpg.at[slot, g], sem.at[slot]))
    return copies


def _attn_kernel(pt_ref, q_ref, kt_ref, v_ref, qs_ref, kn_ref, vn_ref, kt_hbm, v_hbm,
                 lq1_ref, lk1_ref, lq2_ref, lk2_ref, gsub_ref, o_ref, os_ref,
                 qbd_scr, s_scr, p_scr, m_scr, l_scr, a_scr, acc_scr,
                 sqbd_scr, kbuf, vbuf, sm_scr, sl_scr, sacc_scr, kpg, vpg, page_sem,
                 *, n_pg, steps_per_sample, tq, n_heads, qk_dim, page, lam_init):
    i = pl.program_id(2)
    step = (pl.program_id(0) * pl.num_programs(1) + pl.program_id(1)) * pl.num_programs(2) + i
    total_steps = pl.num_programs(0) * pl.num_programs(1) * pl.num_programs(2)
    slot = step % 2
    copies = functools.partial(_page_copies, pt_ref, kt_hbm, v_hbm, kpg, vpg, page_sem,
                               n_pg=n_pg, steps_per_sample=steps_per_sample)

    @pl.when(step == 0)
    def _():
        for c in copies(step, slot):
            c.start()

    @pl.when(step + 1 < total_steps)
    def _():
        for c in copies(step + 1, 1 - slot):
            c.start()

    for c in copies(step, slot):
        c.wait()
    k_refs = [kpg.at[slot, g] for g in range(n_pg)]
    v_refs = [vpg.at[slot, g] for g in range(n_pg)]
    lam = _diff_lambda(lq1_ref, lk1_ref, lq2_ref, lk2_ref, lam_init)
    sample_start, cast_page, reduce_chunk, sample_finish = _sample_attn_parts(
        qs_ref, kn_ref, vn_ref, k_refs, v_refs, lam, gsub_ref, os_ref,
        sqbd_scr, kbuf, vbuf, sm_scr, sl_scr, sacc_scr,
        n_heads=n_heads, qk_dim=qk_dim, page=page, lam_init=lam_init)
    n_chunks = n_pg // PAGE_CHUNK
    sub_step = step % steps_per_sample
    pl.when(sub_step == 0)(sample_start)

    def run_unmasked(key_block):
        assert n_chunks == 1
        casts = [functools.partial(cast_page, 0, g) for g in range(PAGE_CHUNK)]
        for n_fat in range(CAST_SPREAD + 1):
            def branch(n_fat=n_fat):
                if n_fat == 0:
                    for cast in casts:
                        cast()
                for b in range(n_fat):
                    key_block(b, 0, casts[b * len(casts) // n_fat:(b + 1) * len(casts) // n_fat])
                reduce_chunk()
                if n_fat == CAST_SPREAD:
                    lax.fori_loop(CAST_SPREAD, i, key_block, 0)
            pl.when(i >= n_fat if n_fat == CAST_SPREAD else i == n_fat)(branch)

    _prompt_attn_body(i, q_ref, kt_ref, v_ref, lam, gsub_ref, o_ref,
                      qbd_scr, s_scr, p_scr, m_scr, l_scr, a_scr, acc_scr,
                      tq=tq, qk_dim=qk_dim, lam_init=lam_init, row_chunk=32, run_unmasked=run_unmasked)
    pl.when(sub_step == steps_per_sample - 1)(sample_finish)


def _attention(q, kt, v, qs, k_new, v_new, kt_pages, v_pages, page_table, lw, *,
               n_heads, qk_dim, lam_init, tq, n_hg):
    batch, seq, feat = q.shape
    dec_batch, dec_seq, _ = qs.shape
    page = kt_pages.shape[-1]
    n_pages = page_table.shape[1]
    v_dim = feat // n_heads
    n_groups = n_heads // n_hg
    n_q = seq // tq
    total_steps = batch * n_groups * n_q
    assert 2 * qk_dim == LANES and v_dim == LANES and page == LANES and n_heads % n_hg == 0
    assert (dec_batch * n_pages) % total_steps == 0
    n_pg = dec_batch * n_pages // total_steps
    assert n_pages % n_pg == 0 and n_pg % PAGE_CHUNK == 0
    steps_per_sample = n_pages // n_pg
    rows = 2 * tq
    srows = 2 * n_heads * dec_seq
    gw = n_hg * LANES

    def linear(b, h, i):
        return (b * n_groups + h) * n_q + i

    small = lambda n: pl.BlockSpec((1, n), lambda b, h, i, pt: (0, 0))
    new_spec = pl.BlockSpec((None, dec_seq, feat), lambda b, h, i, pt: (linear(b, h, i) // steps_per_sample, 0, 0))

    kern = functools.partial(_attn_kernel, n_pg=n_pg, steps_per_sample=steps_per_sample, tq=tq,
                             n_heads=n_heads, qk_dim=qk_dim, page=page, lam_init=lam_init)
    grid_spec = pltpu.PrefetchScalarGridSpec(
        num_scalar_prefetch=1,
        grid=(batch, n_groups, n_q),
        in_specs=[
            pl.BlockSpec((None, tq, gw), lambda b, h, i, pt: (b, i, h)),
            pl.BlockSpec((None, gw, seq), lambda b, h, i, pt: (b, h, 0), pipeline_mode=pl.Buffered(1)),
            pl.BlockSpec((None, seq, gw), lambda b, h, i, pt: (b, 0, h), pipeline_mode=pl.Buffered(1)),
            new_spec, new_spec, new_spec,
            pl.BlockSpec(memory_space=pl.ANY), pl.BlockSpec(memory_space=pl.ANY)]
        + [small(qk_dim)] * 4 + [small(v_dim)],
        out_specs=[pl.BlockSpec((None, tq, gw), lambda b, h, i, pt: (b, i, h)), new_spec],
        scratch_shapes=[
            pltpu.VMEM((n_hg, rows, 2 * qk_dim), BF16),
            pltpu.VMEM((n_hg, rows, tq), F32),
            pltpu.VMEM((n_hg, rows, tq), BF16),
            pltpu.VMEM((n_hg, rows, LANES), F32),
            pltpu.VMEM((n_hg, rows, LANES), F32),
            pltpu.VMEM((n_hg, rows, LANES), F32),
            pltpu.VMEM((n_hg, rows, v_dim), F32),
            pltpu.VMEM((srows, feat), BF16),
            pltpu.VMEM((feat, PAGE_CHUNK * page), BF16),
            pltpu.VMEM((PAGE_CHUNK * page, feat), BF16),
            pltpu.VMEM((srows, 1), F32),
            pltpu.VMEM((srows, 1), F32),
            pltpu.VMEM((srows, feat), F32),
            pltpu.VMEM((2, n_pg, feat, page), F32),
            pltpu.VMEM((2, n_pg, page * n_heads, v_dim), F32),
            pltpu.SemaphoreType.DMA((2,)),
        ],
    )
    return pl.pallas_call(
        kern,
        grid_spec=grid_spec,
        out_shape=[jax.ShapeDtypeStruct((batch, seq, feat), BF16),
                   jax.ShapeDtypeStruct((dec_batch, dec_seq, feat), F32)],
        compiler_params=pltpu.CompilerParams(
            dimension_semantics=("arbitrary", "arbitrary", "arbitrary"), vmem_limit_bytes=VMEM_LIMIT),
        name="attention",
    )(page_table, q, kt, v, qs, k_new, v_new, kt_pages, v_pages,
      lw['lq1'], lw['lk1'], lw['lq2'], lw['lk2'], lw['g_subln'])


def _mix_out_kernel(x_ref, oa_ref, hs_ref, yr_ref, ga_ref, gr_ref, p_ref,
                    wattn_ref, wrec_ref, wout_ref, gffn_ref, wg_ref, wu_ref, wd_ref,
                    gple_ref, wpg_ref, wpp_ref, gfin_ref, y_ref, *, ff_chunk, final_norm):
    dot = functools.partial(jnp.dot, preferred_element_type=F32)
    o_a = dot(oa_ref[...].astype(BF16), wattn_ref[...])
    o_r = dot((hs_ref[...] * jax.nn.gelu(yr_ref[...])).astype(BF16), wrec_ref[...])
    m = jax.nn.sigmoid(ga_ref[...]) * o_a + jax.nn.sigmoid(gr_ref[...]) * o_r
    h = x_ref[...] + dot(m.astype(BF16), wout_ref[...])

    u2 = _rms(h, gffn_ref[...]).astype(BF16)
    d_ff = wg_ref.shape[1]
    ffn = jnp.zeros_like(h)
    for c0 in range(0, d_ff, ff_chunk):
        c1 = min(c0 + ff_chunk, d_ff)
        hid = jax.nn.silu(dot(u2, wg_ref[:, c0:c1])) * dot(u2, wu_ref[:, c0:c1])
        ffn = ffn + dot(hid.astype(BF16), wd_ref[c0:c1, :])
    h = h + ffn

    g = jax.nn.sigmoid(dot(_rms(h, gple_ref[...]).astype(BF16), wpg_ref[...]))
    h = h + g * dot(p_ref[...].astype(BF16), wpp_ref[...])
    y_ref[...] = _rms(h, gfin_ref[...]) if final_norm else h


def _mix_out(x2d, oa, hs, yr, ga, gr, p2d, lw, g_final, *, tm, final_norm):
    n_tok, d_model = x2d.shape
    tok_spec = lambda w: pl.BlockSpec((tm, w), lambda i: (i, 0))
    weights = [lw['w_attn_br'], lw['w_rec_br'], lw['w_out'], lw['g_ffn'], lw['w_ffn_gate'], lw['w_ffn_up'],
               lw['w_ffn_down'], lw['g_ple'], lw['w_ple_gate'], lw['w_ple_proj'], g_final]
    kern = functools.partial(_mix_out_kernel, ff_chunk=1024, final_norm=final_norm)
    return pl.pallas_call(
        kern,
        grid=(n_tok // tm,),
        in_specs=[tok_spec(d_model), tok_spec(oa.shape[1]), tok_spec(hs.shape[1]), tok_spec(yr.shape[1]),
                  tok_spec(d_model), tok_spec(d_model), tok_spec(p2d.shape[1])]
        + [_resident(w.shape) for w in weights],
        out_specs=tok_spec(d_model),
        out_shape=jax.ShapeDtypeStruct((n_tok, d_model), F32),
        compiler_params=pltpu.CompilerParams(
            dimension_semantics=("arbitrary",), vmem_limit_bytes=VMEM_LIMIT),
        name="mix_out",
    )(x2d, oa, hs, yr, ga, gr, p2d, *weights)


def kernel(x_prompt, x_sample, p_prompt, p_sample, cache_k, cache_v, page_table, state_conv, state_h, g_mix, w_in, lambda_q1, lambda_k1, lambda_q2, lambda_k2, g_subln, w_attn_br, w_conv, b_conv, w_gate_a, b_gate_a, w_gate_x, b_gate_x, lru_lambda, w_rec_br, w_out, g_ffn, w_ffn_gate, w_ffn_up, w_ffn_down, g_ple, w_ple_gate, w_ple_proj, g_final):
    depth = w_in.shape[0]
    batch, seq, d_model = x_prompt.shape
    dec_batch, dec_seq, _ = x_sample.shape
    _, n_phys, page, n_heads, _, qk_dim = cache_k.shape
    v_dim = cache_v.shape[-1]
    width = w_conv.shape[-1]
    feat = n_heads * v_dim
    qk_scale = qk_dim ** -0.5 * math.log2(math.e)

    hp = x_prompt.reshape(batch * seq, d_model)
    hs = x_sample.reshape(dec_batch * dec_seq, d_model)
    outs = [[] for _ in range(8)]
    for l in range(depth):
        lam_init = 0.8 - 0.6 * math.exp(-0.3 * l)
        last = l == depth - 1
        lw = dict(
            g_mix=g_mix[l][None], w_in=w_in[l].astype(BF16),
            lq1=lambda_q1[l][None], lk1=lambda_k1[l][None], lq2=lambda_q2[l][None], lk2=lambda_k2[l][None],
            g_subln=g_subln[l][None], w_attn_br=w_attn_br[l].astype(BF16),
            w_conv=w_conv[l], b_conv=b_conv[l][None],
            w_gate=jnp.concatenate([w_gate_a[l], w_gate_x[l]], axis=-1).astype(BF16),
            b_gate=jnp.concatenate([b_gate_a[l], b_gate_x[l]], axis=-1)[:, None, :],
            lru_lambda=lru_lambda[l][None], w_rec_br=w_rec_br[l].astype(BF16), w_out=w_out[l].astype(BF16),
            g_ffn=g_ffn[l][None], w_ffn_gate=w_ffn_gate[l].astype(BF16), w_ffn_up=w_ffn_up[l].astype(BF16),
            w_ffn_down=w_ffn_down[l].astype(BF16), g_ple=g_ple[l][None],
            w_ple_gate=w_ple_gate[l].astype(BF16), w_ple_proj=w_ple_proj[l].astype(BF16))
        gfin = g_final[None]

        conv0 = jnp.zeros((batch, CONV_WIDTH - 1, width), F32)
        h0 = jnp.zeros((batch, 1, width), F32)
        q, kf, vf, kb, vb, rec, yr, ga, gr, cnew, hnew = _mix_in(
            hp, conv0, h0, lw, batch=batch, seq=seq, nb=1, tt=256, qk_scale=qk_scale, k_transposed=True)
        qs, kfs, vfs, _, _, recs, yrs, gas, grs, cnews, hnews = _mix_in(
            hs, state_conv[l], state_h[l][:, None, :], lw,
            batch=dec_batch, seq=dec_seq, nb=dec_batch, tt=dec_seq, qk_scale=qk_scale, k_transposed=False)

        kt_pages = jnp.transpose(cache_k[l], (0, 2, 3, 4, 1)).reshape(n_phys, feat, page)
        v_pages = cache_v[l].reshape(n_phys, page * n_heads, v_dim)
        oa, oas = _attention(
            q.reshape(batch, seq, feat), kb, vb.reshape(batch, seq, feat),
            qs.astype(F32).reshape(dec_batch, dec_seq, feat),
            kfs.reshape(dec_batch, dec_seq, feat), vfs.reshape(dec_batch, dec_seq, feat),
            kt_pages, v_pages, page_table, lw,
            n_heads=n_heads, qk_dim=qk_dim, lam_init=lam_init, tq=512, n_hg=2)

        hp = _mix_out(hp, oa.reshape(batch * seq, feat), rec, yr, ga, gr,
                      p_prompt[l].reshape(batch * seq, -1), lw, gfin, tm=256, final_norm=last)
        hs = _mix_out(hs, oas.reshape(dec_batch * dec_seq, feat), recs, yrs, gas, grs,
                      p_sample[l].reshape(dec_batch * dec_seq, -1), lw, gfin,
                      tm=dec_batch * dec_seq, final_norm=last)

        outs[0].append(jnp.transpose(kf.reshape(batch, n_heads, 2, qk_dim, seq), (0, 4, 1, 2, 3)))
        outs[1].append(vf.reshape(batch, seq, n_heads, v_dim))
        outs[2].append(cnew)
        outs[3].append(hnew.reshape(batch, width))
        outs[4].append(kfs.reshape(dec_batch, dec_seq, n_heads, 2, qk_dim))
        outs[5].append(vfs.reshape(dec_batch, dec_seq, n_heads, v_dim))
        outs[6].append(cnews)
        outs[7].append(hnews.reshape(dec_batch, width))

    y_prompt = hp.reshape(batch, seq, d_model)
    y_sample = hs.reshape(dec_batch, dec_seq, d_model)
    return (y_prompt, y_sample) + tuple(jnp.stack(o) for o in outs)
```
